```python
import math
import jax, jax.numpy as jnp
from jax import lax
import numpy as np

D_MODEL = 1024
BATCH = 1
SEQ = 16384
DEPTH = 4
DEC_BATCH = 16
DEC_SEQ = 4096
PAST_LEN = 128

N_MIXERS = 4
N_MLA = (DEPTH - 0 + N_MIXERS - 1) // N_MIXERS
N_FNET = (DEPTH - 1 + N_MIXERS - 1) // N_MIXERS
N_SGU = (DEPTH - 2 + N_MIXERS - 1) // N_MIXERS
N_CONV = (DEPTH - 3 + N_MIXERS - 1) // N_MIXERS
EPS = 1e-6
D_FF = 2816
MLA_HEADS = 16
QK_NOPE = 64
QK_ROPE = 32
V_HEAD = 64
Q_LORA = 384
KV_LORA = 256
ROPE_THETA = 10000.0
Q_BLOCK = 128
ATTN_SCALE = 1.0 / math.sqrt(QK_NOPE + QK_ROPE)
FNET_GROUPS = 4
FNET_GC = D_MODEL // FNET_GROUPS
SGU_CHUNK = 128
SGU_HALF = 2 * D_MODEL
SGU_GROUPS = 4
SGU_GC = SGU_HALF // SGU_GROUPS
CONV_WIDTH = 31
CONV_PAD = (CONV_WIDTH - 1) // 2

kernel_name = "hybrid_mla_fnet_sgu_conv_encoder"


def rmsnorm(x, g):
    xf = x.astype(jnp.float32)
    y = xf * lax.rsqrt(jnp.mean(xf * xf, axis=-1, keepdims=True) + EPS)
    return (y * g.astype(jnp.float32)).astype(x.dtype)


def swiglu(h, wg, wu, wd):
    return (jax.nn.silu(h @ wg) * (h @ wu)) @ wd


def rope_tables(seq_len):
    pos = jnp.arange(seq_len, dtype=jnp.float32)
    inv_freq = 1.0 / (ROPE_THETA ** (jnp.arange(0, QK_ROPE, 2, dtype=jnp.float32) / QK_ROPE))
    ang = pos[:, None] * inv_freq[None, :]
    return jnp.cos(ang), jnp.sin(ang)


def apply_rope(x, cos, sin):
    xf = x.astype(jnp.float32)
    x1, x2 = xf[..., : QK_ROPE // 2], xf[..., QK_ROPE // 2:]
    out = jnp.concatenate([x1 * cos - x2 * sin, x2 * cos + x1 * sin], axis=-1)
    return out.astype(x.dtype)


def mla_mixer(h, wq_a, gq, wq_b, wkv_a, gkv, wkv_b, wo):
    B, S, _ = h.shape
    cq = rmsnorm(h @ wq_a, gq)
    q = (cq @ wq_b).reshape(B, S, MLA_HEADS, QK_NOPE + QK_ROPE)
    q_nope, q_rope = q[..., :QK_NOPE], q[..., QK_NOPE:]
    kv = h @ wkv_a
    ckv = rmsnorm(kv[..., :KV_LORA], gkv)
    k_rope = kv[..., KV_LORA:]
    kvb = (ckv @ wkv_b).reshape(B, S, MLA_HEADS, QK_NOPE + V_HEAD)
    k_nope, v = kvb[..., :QK_NOPE], kvb[..., QK_NOPE:]
    cos, sin = rope_tables(S)
    q_rope = apply_rope(q_rope, cos[None, :, None, :], sin[None, :, None, :])
    k_rope = apply_rope(k_rope, cos[None], sin[None])
    nq = S // Q_BLOCK
    qn_b = q_nope.reshape(B, nq, Q_BLOCK, MLA_HEADS, QK_NOPE).transpose(1, 0, 2, 3, 4)
    qr_b = q_rope.reshape(B, nq, Q_BLOCK, MLA_HEADS, QK_ROPE).transpose(1, 0, 2, 3, 4)

    def block(args):
        qn, qr = args
        s = jnp.einsum('bqhd,bkhd->bhqk', qn, k_nope) + jnp.einsum('bqhr,bkr->bhqk', qr, k_rope)
        p = jax.nn.softmax(s.astype(jnp.float32) * ATTN_SCALE, axis=-1).astype(v.dtype)
        return jnp.einsum('bhqk,bkhd->bqhd', p, v)

    o = lax.map(block, (qn_b, qr_b))
    o = o.transpose(1, 0, 2, 3, 4).reshape(B, S, MLA_HEADS * V_HEAD)
    return o @ wo


def fnet_mixer(h, wo, bo):
    B, S, D = h.shape
    hg = h.astype(jnp.float32).reshape(B, S, FNET_GROUPS, FNET_GC)
    y = jnp.fft.fft2(hg, axes=(1, 3), norm='ortho').real
    y = y.reshape(B, S, D).astype(h.dtype)
    return y @ wo + bo


def sgu_mixer(h, w_in, gv, ws, bs, w_out):
    B, S, _ = h.shape
    z = jax.nn.gelu(h @ w_in)
    u, v = z[..., :SGU_HALF], z[..., SGU_HALF:]
    v = rmsnorm(v, gv)
    vc = v.reshape(B, S // SGU_CHUNK, SGU_CHUNK, SGU_GROUPS, SGU_GC)
    sv = jnp.einsum('gts,bcsge->bctge', ws, vc) + bs.T[None, None, :, :, None]
    return (u * sv.reshape(B, S, SGU_HALF)) @ w_out


def conv_mixer(h, w_pw1, b_pw1, w_dw, b_dw, gn, w_pw2, b_pw2):
    a = h @ w_pw1 + b_pw1
    g = a[..., :D_MODEL] * jax.nn.sigmoid(a[..., D_MODEL:])
    dw = lax.conv_general_dilated(
        g, w_dw, window_strides=(1,), padding=[(CONV_PAD, CONV_PAD)],
        dimension_numbers=('NWC', 'WIO', 'NWC'), feature_group_count=D_MODEL) + b_dw
    return jax.nn.silu(rmsnorm(dw, gn)) @ w_pw2 + b_pw2


def trunk(x, norm_g, final_g, ffn1_wg, ffn1_wu, ffn1_wd, ffn2_wg, ffn2_wu, ffn2_wd,
          mla_wq_a, mla_gq, mla_wq_b, mla_wkv_a, mla_gkv, mla_wkv_b, mla_wo,
          fnet_wo, fnet_bo, sgu_w_in, sgu_gv, sgu_ws, sgu_bs, sgu_w_out,
          conv_w_pw1, conv_b_pw1, conv_w_dw, conv_b_dw, conv_gn, conv_w_pw2, conv_b_pw2):
    for i in range(DEPTH):
        j = i // N_MIXERS
        kind = i % N_MIXERS
        x = x + 0.5 * swiglu(rmsnorm(x, norm_g[i, 0]), ffn1_wg[i], ffn1_wu[i], ffn1_wd[i])
        h = rmsnorm(x, norm_g[i, 1])
        if kind == 0:
            m = mla_mixer(h, mla_wq_a[j], mla_gq[j], mla_wq_b[j], mla_wkv_a[j], mla_gkv[j], mla_wkv_b[j], mla_wo[j])
        elif kind == 1:
            m = fnet_mixer(h, fnet_wo[j], fnet_bo[j])
        elif kind == 2:
            m = sgu_mixer(h, sgu_w_in[j], sgu_gv[j], sgu_ws[j], sgu_bs[j], sgu_w_out[j])
        else:
            m = conv_mixer(h, conv_w_pw1[j], conv_b_pw1[j], conv_w_dw[j], conv_b_dw[j], conv_gn[j], conv_w_pw2[j], conv_b_pw2[j])
        x = x + m
        x = x + 0.5 * swiglu(rmsnorm(x, norm_g[i, 2]), ffn2_wg[i], ffn2_wu[i], ffn2_wd[i])
    return rmsnorm(x, final_g)


def setup_inputs(seed: int = 0) -> dict:
    key = jax.random.key(seed)
    ks = iter(jax.random.split(key, 64))

    def w(shape, fan_in):
        return jax.random.normal(next(ks), shape, jnp.float32) * (fan_in ** -0.5)

    def gain(shape):
        return 1.0 + 0.05 * jax.random.normal(next(ks), shape, jnp.float32)

    def bias(shape):
        return 0.02 * jax.random.normal(next(ks), shape, jnp.float32)

    D = D_MODEL
    return {
        'x_prompt': jax.random.normal(next(ks), (BATCH, SEQ, D), jnp.float32),
        'x_sample': jax.random.normal(next(ks), (DEC_BATCH, DEC_SEQ, D), jnp.float32),
        'norm_g': gain((DEPTH, 3, D)),
        'final_g': gain((D,)),
        'ffn1_wg': w((DEPTH, D, D_FF), D),
        'ffn1_wu': w((DEPTH, D, D_FF), D),
        'ffn1_wd': w((DEPTH, D_FF, D), D_FF),
        'ffn2_wg': w((DEPTH, D, D_FF), D),
        'ffn2_wu': w((DEPTH, D, D_FF), D),
        'ffn2_wd': w((DEPTH, D_FF, D), D_FF),
        'mla_wq_a': w((N_MLA, D, Q_LORA), D),
        'mla_gq': gain((N_MLA, Q_LORA)),
        'mla_wq_b': w((N_MLA, Q_LORA, MLA_HEADS * (QK_NOPE + QK_ROPE)), Q_LORA),
        'mla_wkv_a': w((N_MLA, D, KV_LORA + QK_ROPE), D),
        'mla_gkv': gain((N_MLA, KV_LORA)),
        'mla_wkv_b': w((N_MLA, KV_LORA, MLA_HEADS * (QK_NOPE + V_HEAD)), KV_LORA),
        'mla_wo': w((N_MLA, MLA_HEADS * V_HEAD, D), MLA_HEADS * V_HEAD),
        'fnet_wo': w((N_FNET, D, D), D),
        'fnet_bo': bias((N_FNET, D)),
        'sgu_w_in': w((N_SGU, D, 2 * SGU_HALF), D),
        'sgu_gv': gain((N_SGU, SGU_HALF)),
        'sgu_ws': w((N_SGU, SGU_GROUPS, SGU_CHUNK, SGU_CHUNK), SGU_CHUNK),
        'sgu_bs': 1.0 + 0.01 * jax.random.normal(next(ks), (N_SGU, SGU_GROUPS, SGU_CHUNK), jnp.float32),
        'sgu_w_out': w((N_SGU, SGU_HALF, D), SGU_HALF),
        'conv_w_pw1': w((N_CONV, D, 2 * D), D),
        'conv_b_pw1': bias((N_CONV, 2 * D)),
        'conv_w_dw': w((N_CONV, CONV_WIDTH, 1, D), CONV_WIDTH),
        'conv_b_dw': bias((N_CONV, D)),
        'conv_gn': gain((N_CONV, D)),
        'conv_w_pw2': w((N_CONV, D, D), D),
        'conv_b_pw2': bias((N_CONV, D)),
    }


def reference(x_prompt, x_sample, norm_g, final_g, ffn1_wg, ffn1_wu, ffn1_wd, ffn2_wg, ffn2_wu, ffn2_wd,
              mla_wq_a, mla_gq, mla_wq_b, mla_wkv_a, mla_gkv, mla_wkv_b, mla_wo,
              fnet_wo, fnet_bo, sgu_w_in, sgu_gv, sgu_ws, sgu_bs, sgu_w_out,
              conv_w_pw1, conv_b_pw1, conv_w_dw, conv_b_dw, conv_gn, conv_w_pw2, conv_b_pw2):
    weights = (norm_g, final_g, ffn1_wg, ffn1_wu, ffn1_wd, ffn2_wg, ffn2_wu, ffn2_wd,
               mla_wq_a, mla_gq, mla_wq_b, mla_wkv_a, mla_gkv, mla_wkv_b, mla_wo,
               fnet_wo, fnet_bo, sgu_w_in, sgu_gv, sgu_ws, sgu_bs, sgu_w_out,
               conv_w_pw1, conv_b_pw1, conv_w_dw, conv_b_dw, conv_gn, conv_w_pw2, conv_b_pw2)
    y_prompt = trunk(x_prompt, *weights)
    y_sample = trunk(x_sample, *weights)
    return (y_prompt, y_sample)
```

```python
import functools
import math

import numpy as np
import jax
import jax.numpy as jnp
from jax import lax
from jax.experimental import pallas as pl
from jax.experimental.pallas import tpu as pltpu

F32 = jnp.float32
BF16 = jnp.bfloat16

EPS = 1e-6
D_MODEL = 1024
D_FF = 2816
FF_CHUNKS = (1024, 1024, 768)
HEADS = 16
QK_NOPE = 64
QK_ROPE = 32
V_HEAD = 64
Q_LORA = 384
KV_LORA = 256
ROPE_THETA = 10000.0
ATTN_SCALE = 1.0 / math.sqrt(QK_NOPE + QK_ROPE)
HEAD_PAD = 128
FNET_GROUPS = 4
FNET_GC = D_MODEL // FNET_GROUPS
SGU_CHUNK = 128
SGU_HALF = 2 * D_MODEL
SGU_GROUPS = 4
SGU_GC = SGU_HALF // SGU_GROUPS
CONV_WIDTH = 31
CONV_PAD = (CONV_WIDTH - 1) // 2
HALO = 16
SUBLANES = 8
LANES = 128
MIB = 1 << 20


def _params(n_axes, vmem_mib):
    return pltpu.CompilerParams(
        dimension_semantics=("arbitrary",) * n_axes,
        vmem_limit_bytes=vmem_mib * MIB)


def _resident(shape):
    nd = len(shape)
    return pl.BlockSpec(shape, lambda *_: (0,) * nd, pipeline_mode=pl.Buffered(1))


def _rms(x, g):
    return x * lax.rsqrt(jnp.mean(x * x, axis=-1, keepdims=True) + EPS) * g


def _dot(a, b):
    return jnp.dot(a, b, preferred_element_type=F32)


def _ffn_body(x, g_ref, wg_ref, wu_ref, wd_ref):
    h = _rms(x, g_ref[...]).astype(BF16)
    acc = None
    off = 0
    for c in FF_CHUNKS:
        gate = _dot(h, wg_ref[:, off:off + c])
        up = _dot(h, wu_ref[:, off:off + c])
        a = (gate * jax.nn.sigmoid(gate) * up).astype(BF16)
        part = _dot(a, wd_ref[off:off + c, :])
        acc = part if acc is None else acc + part
        off += c
    return x + 0.5 * acc


def _ffn_kernel(x_ref, g_ref, wg_ref, wu_ref, wd_ref, o_ref):
    o_ref[...] = _ffn_body(x_ref[...], g_ref, wg_ref, wu_ref, wd_ref)


def _ffn_final_kernel(x_ref, g_ref, wg_ref, wu_ref, wd_ref, fg_ref, o_ref):
    y = _ffn_body(x_ref[...], g_ref, wg_ref, wu_ref, wd_ref)
    o_ref[...] = _rms(y, fg_ref[...])


def _ffn_proj_kernel(x_ref, o_in_ref, wo_ref, g_ref, wg_ref, wu_ref, wd_ref, o_ref):
    x = x_ref[...] + _dot(o_in_ref[...], wo_ref[...])
    o_ref[...] = _ffn_body(x, g_ref, wg_ref, wu_ref, wd_ref)


def _ffn(x, g, wg, wu, wd, *, tm, row_start=0, n_rows=None, final_g=None, pre=None):
    T, D = x.shape
    n_rows = T if n_rows is None else n_rows
    assert n_rows % tm == 0 and row_start % tm == 0
    blk0 = row_start // tm
    row_spec = pl.BlockSpec((tm, D), lambda i: (blk0 + i, 0))
    w_specs = [_resident((1, D)), _resident(wg.shape), _resident(wu.shape), _resident(wd.shape)]
    args = [x]
    in_specs = [row_spec]
    if pre is not None:
        o_in, wo = pre
        args += [o_in, wo]
        in_specs += [pl.BlockSpec((tm, o_in.shape[1]), lambda i: (blk0 + i, 0)), _resident(wo.shape)]
        body = _ffn_proj_kernel
    else:
        body = _ffn_kernel
    args += [g.reshape(1, D), wg, wu, wd]
    in_specs += w_specs
    if final_g is not None:
        assert pre is None
        args.append(final_g.reshape(1, D))
        in_specs.append(_resident((1, D)))
        body = _ffn_final_kernel
    return pl.pallas_call(
        body,
        grid=(n_rows // tm,),
        in_specs=in_specs,
        out_specs=pl.BlockSpec((tm, D), lambda i: (i, 0)),
        out_shape=jax.ShapeDtypeStruct((n_rows, D), F32),
        compiler_params=_params(1, 52),
        name="ffn",
    )(*args)


def _mla_proj_kernel(x_ref, g_ref, wqa_ref, gq_ref, wq1_ref, wq2_ref, wkva_ref, gkv_ref,
                     wkr_ref, wk_ref, wv_ref, cq_ref, sq_ref, ck_ref, sk_ref,
                     q_ref, k_ref, v_ref):
    h = _rms(x_ref[...], g_ref[...]).astype(BF16)
    cq = _rms(_dot(h, wqa_ref[...]), gq_ref[...]).astype(BF16)
    ckv = _rms(_dot(h, wkva_ref[...]), gkv_ref[...]).astype(BF16)
    kr2 = _dot(h, wkr_ref[...])
    kr = kr2[:, :HEAD_PAD] * ck_ref[...] + kr2[:, HEAD_PAD:] * sk_ref[...]
    cq_t = cq_ref[...]
    sq_t = sq_ref[...]
    for hd in range(HEADS):
        sl = slice(hd * HEAD_PAD, (hd + 1) * HEAD_PAD)
        qh = _dot(cq, wq1_ref[:, sl]) * cq_t + _dot(cq, wq2_ref[:, sl]) * sq_t
        q_ref[hd] = qh.astype(BF16)
        k_ref[hd] = (_dot(ckv, wk_ref[:, sl]) + kr).astype(BF16)
    v = _dot(ckv, wv_ref[...])
    for hp in range(HEADS // 2):
        v_ref[hp] = v[:, hp * LANES:(hp + 1) * LANES].astype(BF16)


def _attn_kernel(q_ref, k_ref, v_ref, o_ref, *, seq, tk):
    tq = q_ref.shape[1]
    outs = []
    for j in range(2):
        q = q_ref[j]

        def body(c, carry, j=j, q=q):
            m, l, acc = carry
            start = pl.multiple_of(c * tk, tk)
            ks = k_ref[j, pl.ds(start, tk), :]
            vs = v_ref[0, pl.ds(start, tk), :]
            s = lax.dot_general(q, ks, (((1,), (1,)), ((), ())), preferred_element_type=F32)
            m_new = jnp.maximum(m, jnp.max(s, axis=-1, keepdims=True))
            alpha = jnp.exp(m - m_new)
            p = jnp.exp(s - m_new)
            l = alpha * l + jnp.sum(p, axis=-1, keepdims=True)
            acc = alpha * acc + _dot(p.astype(BF16), vs)
            return m_new, l, acc

        init = (jnp.full((tq, 1), -jnp.inf, F32), jnp.zeros((tq, 1), F32), jnp.zeros((tq, LANES), F32))
        m, l, acc = lax.fori_loop(0, seq // tk, body, init)
        outs.append(acc / l)
    lane = lax.broadcasted_iota(jnp.int32, (tq, LANES), 1)
    o_ref[...] = jnp.where(lane < V_HEAD, outs[0], outs[1]).astype(o_ref.dtype)


def _rope_tables(seq_len):
    pos = jnp.arange(seq_len, dtype=F32)
    inv_freq = 1.0 / (ROPE_THETA ** (jnp.arange(0, QK_ROPE, 2, dtype=F32) / QK_ROPE))
    ang = pos[:, None] * inv_freq[None, :]
    cos, sin = jnp.cos(ang), jnp.sin(ang)
    ones = jnp.ones((seq_len, QK_NOPE), F32)
    zeros_n = jnp.zeros((seq_len, QK_NOPE), F32)
    zeros_p = jnp.zeros((seq_len, HEAD_PAD - QK_NOPE - QK_ROPE), F32)
    c_tab = jnp.concatenate([ones, cos, cos, zeros_p], axis=1)
    s_tab = jnp.concatenate([zeros_n, -sin, sin, zeros_p], axis=1)
    return c_tab, s_tab


def _swap_halves(w):
    half = w.shape[-1] // 2
    return jnp.concatenate([w[..., half:], w[..., :half]], axis=-1)


def _mla(x, groups, g, wq_a, gq, wq_b, wkv_a, gkv, wkv_b, *, tm, tq, tk):
    T, D = x.shape
    pad = HEAD_PAD - QK_NOPE - QK_ROPE
    wq3 = wq_b.reshape(Q_LORA, HEADS, QK_NOPE + QK_ROPE)
    qn, qr = wq3[..., :QK_NOPE], wq3[..., QK_NOPE:]
    zp = jnp.zeros((Q_LORA, HEADS, pad), F32)
    wq1 = jnp.concatenate([qn, qr, zp], -1).reshape(Q_LORA, HEADS * HEAD_PAD).astype(BF16)
    wq2 = jnp.concatenate([jnp.zeros_like(qn), _swap_halves(qr), zp], -1).reshape(Q_LORA, HEADS * HEAD_PAD).astype(BF16)
    wkva_c = wkv_a[:, :KV_LORA].astype(BF16)
    wr = wkv_a[:, KV_LORA:]
    zn = jnp.zeros((D, QK_NOPE), F32)
    zpd = jnp.zeros((D, pad), F32)
    wkr = jnp.concatenate([zn, wr, zpd, zn, _swap_halves(wr), zpd], axis=1).astype(BF16)
    wkv3 = wkv_b.reshape(KV_LORA, HEADS, QK_NOPE + V_HEAD)
    wk = jnp.concatenate([wkv3[..., :QK_NOPE], jnp.zeros((KV_LORA, HEADS, HEAD_PAD - QK_NOPE), F32)], -1)
    wk = wk.reshape(KV_LORA, HEADS * HEAD_PAD).astype(BF16)
    wv = wkv3[..., QK_NOPE:].reshape(KV_LORA, HEADS * V_HEAD).astype(BF16)

    s_max = max(s for _, _, s in groups)
    c_tab, s_tab = _rope_tables(s_max)
    cq_tab, sq_tab = c_tab * ATTN_SCALE, s_tab * ATTN_SCALE

    bounds = [(off // tm, (off + b * s) // tm, s // tm) for off, b, s in groups]

    def pos_block(i):
        blk = i
        for lo, _, per in bounds:
            blk = jnp.where(i >= lo, (i - lo) % per, blk)
        return (blk, 0)

    for off, b, s in groups:
        assert s % tm == 0 and off % tm == 0
    tab_spec = pl.BlockSpec((tm, HEAD_PAD), pos_block)
    q, k, v = pl.pallas_call(
        _mla_proj_kernel,
        grid=(T // tm,),
        in_specs=[pl.BlockSpec((tm, D), lambda i: (i, 0)), _resident((1, D)),
                  _resident(wq_a.shape), _resident((1, Q_LORA)), _resident(wq1.shape), _resident(wq2.shape),
                  _resident(wkva_c.shape), _resident((1, KV_LORA)), _resident(wkr.shape),
                  _resident(wk.shape), _resident(wv.shape),
                  tab_spec, tab_spec, tab_spec, tab_spec],
        out_specs=[pl.BlockSpec((HEADS, tm, HEAD_PAD), lambda i: (0, i, 0)),
                   pl.BlockSpec((HEADS, tm, HEAD_PAD), lambda i: (0, i, 0)),
                   pl.BlockSpec((HEADS // 2, tm, LANES), lambda i: (0, i, 0))],
        out_shape=[jax.ShapeDtypeStruct((HEADS, T, HEAD_PAD), BF16),
                   jax.ShapeDtypeStruct((HEADS, T, HEAD_PAD), BF16),
                   jax.ShapeDtypeStruct((HEADS // 2, T, LANES), BF16)],
        compiler_params=_params(1, 48),
        name="mla_proj",
    )(x, g.reshape(1, D), wq_a.astype(BF16), gq.reshape(1, Q_LORA), wq1, wq2,
      wkva_c, gkv.reshape(1, KV_LORA), wkr, wk, wv, cq_tab, sq_tab, c_tab, s_tab)

    o = None
    for off, nb, s in groups:
        assert off % s == 0 and s % tq == 0 and s % tk == 0
        seq0 = off // s
        nq = s // tq
        q0 = off // tq
        in_specs = [pl.BlockSpec((2, tq, HEAD_PAD), lambda b, hp, qi: (hp, q0 + b * nq + qi, 0)),
                    pl.BlockSpec((2, s, HEAD_PAD), lambda b, hp, qi: (hp, seq0 + b, 0)),
                    pl.BlockSpec((1, s, LANES), lambda b, hp, qi: (hp, seq0 + b, 0))]
        args = [q, k, v]
        aliases = {}
        if o is not None:
            in_specs.append(pl.BlockSpec(memory_space=pl.ANY))
            args.append(o)
            aliases = {3: 0}
        kern = functools.partial(_attn_kernel, seq=s, tk=tk)
        if o is not None:
            kern = functools.partial(_attn_kernel_aliased, seq=s, tk=tk)
        o = pl.pallas_call(
            kern,
            grid=(nb, HEADS // 2, nq),
            in_specs=in_specs,
            out_specs=pl.BlockSpec((tq, LANES), lambda b, hp, qi: (q0 + b * nq + qi, hp)),
            out_shape=jax.ShapeDtypeStruct((T, HEADS * V_HEAD), BF16),
            input_output_aliases=aliases,
            compiler_params=_params(3, 48),
            name="mla_attn",
        )(*args)
    return o


def _attn_kernel_aliased(q_ref, k_ref, v_ref, prev_ref, o_ref, *, seq, tk):
    del prev_ref
    _attn_kernel(q_ref, k_ref, v_ref, o_ref, seq=seq, tk=tk)


def _proj_res_kernel(x_ref, a_ref, w_ref, o_ref):
    o_ref[...] = x_ref[...] + _dot(a_ref[...], w_ref[...])


def _proj_res(x, a, w, *, tm):
    T, D = x.shape
    return pl.pallas_call(
        _proj_res_kernel,
        grid=(T // tm,),
        in_specs=[pl.BlockSpec((tm, D), lambda i: (i, 0)),
                  pl.BlockSpec((tm, a.shape[1]), lambda i: (i, 0)), _resident(w.shape)],
        out_specs=pl.BlockSpec((tm, D), lambda i: (i, 0)),
        out_shape=jax.ShapeDtypeStruct((T, D), F32),
        compiler_params=_params(1, 40),
        name="proj_res",
    )(x, a, w)


def _dft_cos_sin(n):
    idx = np.arange(n)
    ang = 2.0 * np.pi * ((idx[:, None] * idx[None, :]) % n) / n
    scale = 1.0 / math.sqrt(n)
    return np.cos(ang) * scale, np.sin(ang) * scale


def _fnet_factor(s):
    na = 1 << (int(math.log2(s)) // 2)
    nb = s // na
    assert na * nb == s and nb % (2 * SUBLANES) == 0 and na % SUBLANES == 0
    return na, nb


def _fnet_stage1_kernel(x_ref, g_ref, cc_ref, nsc_ref, k1c_ref, k1s_ref, twc_ref, tws_ref,
                        ar_ref, ai_ref):
    na = x_ref.shape[0]
    rows = na * SUBLANES
    halves_r, halves_i = [], []
    for half in range(2):
        sub = slice(half * SUBLANES, (half + 1) * SUBLANES)
        xb = x_ref[:, sub, :].reshape(rows, D_MODEL)
        h = _rms(xb, g_ref[...]).astype(BF16)
        twc = twc_ref[:, sub, :].reshape(rows, LANES)
        tws = tws_ref[:, sub, :].reshape(rows, LANES)
        twc = jnp.concatenate([twc] * (FNET_GC // LANES), axis=1)
        tws = jnp.concatenate([tws] * (FNET_GC // LANES), axis=1)
        cols_r, cols_i = [], []
        for grp in range(FNET_GROUPS):
            hg = h[:, grp * FNET_GC:(grp + 1) * FNET_GC]
            zr = _dot(hg, cc_ref[...]).astype(BF16)
            zi = _dot(hg, nsc_ref[...]).astype(BF16)
            k1c = k1c_ref[...]
            k1s = k1s_ref[...]
            a_r = _dot(k1c, zr) + _dot(k1s, zi)
            a_i = _dot(k1c, zi) - _dot(k1s, zr)
            cols_r.append(a_r * twc + a_i * tws)
            cols_i.append(a_i * twc - a_r * tws)
        halves_r.append(jnp.concatenate(cols_r, axis=1).reshape(na, SUBLANES, D_MODEL))
        halves_i.append(jnp.concatenate(cols_i, axis=1).reshape(na, SUBLANES, D_MODEL))
    ar_ref[...] = jnp.concatenate(halves_r, axis=1).astype(BF16)
    ai_ref[...] = jnp.concatenate(halves_i, axis=1).astype(BF16)


def _fnet_stage2_kernel(x_ref, ar_ref, ai_ref, gc_ref, gs_ref, wo_ref, bo_ref, o_ref):
    nb = x_ref.shape[0]
    rows = nb * SUBLANES
    y = _dot(gc_ref[...], ar_ref[...]) + _dot(gs_ref[...], ai_ref[...])
    m = _dot(y.astype(BF16), wo_ref[...]) + bo_ref[...]
    o_ref[...] = (x_ref[...].reshape(rows, D_MODEL) + m).reshape(nb, SUBLANES, D_MODEL)


def _fnet(x, groups, g, wo, bo):
    T, D = x.shape
    cc, sc = _dft_cos_sin(FNET_GC)
    cc_j = jnp.asarray(cc, F32).astype(BF16)
    nsc_j = jnp.asarray(-sc, F32).astype(BF16)
    wo_b = wo.astype(BF16)
    for off, nbatch, s in groups:
        assert off % s == 0
        na, nb = _fnet_factor(s)
        seq0 = off // s
        c1, s1 = _dft_cos_sin(na)
        eye = np.eye(SUBLANES)
        k1c = jnp.asarray(np.kron(c1, eye), F32).astype(BF16)
        k1s = jnp.asarray(np.kron(s1, eye), F32).astype(BF16)
        c2, s2 = _dft_cos_sin(nb)
        gc = np.zeros((nb, SUBLANES, SUBLANES, nb))
        gs = np.zeros((nb, SUBLANES, SUBLANES, nb))
        for j in range(SUBLANES):
            gc[:, j, j, :] = c2
            gs[:, j, j, :] = s2
        n2 = nb * SUBLANES
        gc_j = jnp.asarray(gc.reshape(n2, n2), F32).astype(BF16)
        gs_j = jnp.asarray(gs.reshape(n2, n2), F32).astype(BF16)
        k1i = jnp.arange(na, dtype=jnp.int32)[:, None]
        bi = jnp.arange(nb, dtype=jnp.int32)[None, :]
        ang = ((k1i * bi) % s).astype(F32) * (2.0 * math.pi / s)
        twc = jnp.broadcast_to(jnp.cos(ang)[:, :, None], (na, nb, LANES))
        tws = jnp.broadcast_to(jnp.sin(ang)[:, :, None], (na, nb, LANES))

        tb = 2 * SUBLANES
        n1 = na * SUBLANES
        x3 = x.reshape(T // nb, nb, D)
        blk3 = pl.BlockSpec((na, tb, D), lambda b, j: (seq0 + b, j, 0))
        tw_spec = pl.BlockSpec((na, tb, LANES), lambda b, j: (0, j, 0))
        a_r, a_i = pl.pallas_call(
            _fnet_stage1_kernel,
            grid=(nbatch, nb // tb),
            in_specs=[blk3, _resident((1, D)), _resident(cc_j.shape), _resident(nsc_j.shape),
                      _resident((n1, n1)), _resident((n1, n1)), tw_spec, tw_spec],
            out_specs=[pl.BlockSpec((na, tb, D), lambda b, j: (b, j, 0))] * 2,
            out_shape=[jax.ShapeDtypeStruct((nbatch * na, nb, D), BF16)] * 2,
            compiler_params=_params(2, 56),
            name="fnet_stage1",
        )(x3, g.reshape(1, D), cc_j, nsc_j, k1c, k1s, twc, tws)

        a_r = a_r.reshape(nbatch * s, D)
        a_i = a_i.reshape(nbatch * s, D)
        xo = x.reshape(T // na, na, D)
        xblk = pl.BlockSpec((nb, SUBLANES, D), lambda b, j: (seq0 + b, j, 0))
        ablk = pl.BlockSpec((n2, D), lambda b, j: (b * (na // SUBLANES) + j, 0))
        xo = pl.pallas_call(
            _fnet_stage2_kernel,
            grid=(nbatch, na // SUBLANES),
            in_specs=[xblk, ablk, ablk, _resident((n2, n2)), _resident((n2, n2)),
                      _resident(wo_b.shape), _resident((1, D))],
            out_specs=xblk,
            out_shape=jax.ShapeDtypeStruct(xo.shape, F32),
            input_output_aliases={0: 0},
            compiler_params=_params(2, 56),
            name="fnet_stage2",
        )(xo, a_r, a_i, gc_j, gs_j, wo_b, bo.reshape(1, D))
        x = xo.reshape(T, D)
    return x


def _gelu_tanh(x):
    return 0.5 * x * (1.0 + jnp.tanh(math.sqrt(2.0 / math.pi) * (x + 0.044715 * (x * x * x))))


def _sgu_kernel(x_ref, g_ref, wu_ref, wv_ref, gv_ref, ws_ref, bs_ref, wout_ref, o_ref):
    x = x_ref[...]
    tm = x.shape[0]
    h = _rms(x, g_ref[...]).astype(BF16)
    v = _gelu_tanh(_dot(h, wv_ref[...]))
    v = _rms(v, gv_ref[...]).astype(BF16)
    u = _gelu_tanh(_dot(h, wu_ref[...]))
    rows = []
    for c in range(tm // SGU_CHUNK):
        r = slice(c * SGU_CHUNK, (c + 1) * SGU_CHUNK)
        cols = []
        for grp in range(SGU_GROUPS):
            cs = slice(grp * SGU_GC, (grp + 1) * SGU_GC)
            sv = _dot(ws_ref[grp], v[r, cs]) + bs_ref[grp]
            cols.append((u[r, cs] * sv).astype(BF16))
        rows.append(jnp.concatenate(cols, axis=1))
    gated = jnp.concatenate(rows, axis=0)
    o_ref[...] = x + _dot(gated, wout_ref[...])


def _sgu(x, g, w_in, gv, ws, bs, w_out, *, tm):
    T, D = x.shape
    assert tm % SGU_CHUNK == 0
    wu = w_in[:, :SGU_HALF].astype(BF16)
    wv = w_in[:, SGU_HALF:].astype(BF16)
    bs_b = jnp.broadcast_to(bs[:, :, None], (SGU_GROUPS, SGU_CHUNK, SGU_GC)).astype(F32)
    return pl.pallas_call(
        _sgu_kernel,
        grid=(T // tm,),
        in_specs=[pl.BlockSpec((tm, D), lambda i: (i, 0)), _resident((1, D)),
                  _resident(wu.shape), _resident(wv.shape), _resident((1, SGU_HALF)),
                  _resident(ws.shape), _resident(bs_b.shape), _resident(w_out.shape)],
        out_specs=pl.BlockSpec((tm, D), lambda i: (i, 0)),
        out_shape=jax.ShapeDtypeStruct((T, D), F32),
        compiler_params=_params(1, 52),
        name="sgu",
    )(x, g.reshape(1, D), wu, wv, gv.reshape(1, SGU_HALF), ws.astype(BF16), bs_b, w_out.astype(BF16))


def _conv_kernel(x_ref, xp_ref, xn_ref, g_ref, w1_ref, b1_ref, wdw_ref, bdw_ref, gn_ref, w2_ref, b2_ref,
                 o_ref, glu_ref, *, seq_starts, seq_ends):
    tm = x_ref.shape[0]
    i = pl.program_id(0)
    start = i * tm
    is_first = functools.reduce(jnp.logical_or, [start == s for s in seq_starts])
    is_last = functools.reduce(jnp.logical_or, [start + tm == e for e in seq_ends])
    x = x_ref[...]
    xa = jnp.concatenate([xp_ref[...], x, xn_ref[...]], axis=0)
    h = _rms(xa, g_ref[...]).astype(BF16)
    a = _dot(h, w1_ref[...]) + b1_ref[...]
    glu = a[:, :D_MODEL] * jax.nn.sigmoid(a[:, D_MODEL:])
    row = lax.broadcasted_iota(jnp.int32, (tm + 2 * HALO, 1), 0)
    outside = jnp.logical_or(jnp.logical_and(is_first, row < HALO),
                             jnp.logical_and(is_last, row >= tm + HALO))
    glu_ref[...] = jnp.where(outside, 0.0, glu)
    acc = jnp.zeros((tm, D_MODEL), F32) + bdw_ref[...]
    for j in range(CONV_WIDTH):
        acc = acc + glu_ref[pl.ds(HALO - CONV_PAD + j, tm), :] * wdw_ref[j:j + 1, :]
    y = _rms(acc, gn_ref[...])
    y = (y * jax.nn.sigmoid(y)).astype(BF16)
    o_ref[...] = x + _dot(y, w2_ref[...]) + b2_ref[...]


def _conv(x, groups, g, w_pw1, b_pw1, w_dw, b_dw, gn, w_pw2, b_pw2, *, tm):
    T, D = x.shape
    seq_starts = tuple(off + b * s for off, nb, s in groups for b in range(nb))
    seq_ends = tuple(off + (b + 1) * s for off, nb, s in groups for b in range(nb))
    for off, nb, s in groups:
        assert s % tm == 0 and off % tm == 0
    r = tm // HALO
    last_halo = T // HALO - 1
    kern = functools.partial(_conv_kernel, seq_starts=seq_starts, seq_ends=seq_ends)
    return pl.pallas_call(
        kern,
        grid=(T // tm,),
        in_specs=[pl.BlockSpec((tm, D), lambda i: (i, 0)),
                  pl.BlockSpec((HALO, D), lambda i: (jnp.maximum(i * r - 1, 0), 0)),
                  pl.BlockSpec((HALO, D), lambda i: (jnp.minimum((i + 1) * r, last_halo), 0)),
                  _resident((1, D)), _resident(w_pw1.shape), _resident((1, 2 * D)),
                  _resident((CONV_WIDTH, D)), _resident((1, D)), _resident((1, D)),
                  _resident(w_pw2.shape), _resident((1, D))],
        out_specs=pl.BlockSpec((tm, D), lambda i: (i, 0)),
        out_shape=jax.ShapeDtypeStruct((T, D), F32),
        scratch_shapes=[pltpu.VMEM((tm + 2 * HALO, D), F32)],
        compiler_params=_params(1, 48),
        name="conv",
    )(x, x, x, g.reshape(1, D), w_pw1.astype(BF16), b_pw1.reshape(1, 2 * D),
      w_dw.reshape(CONV_WIDTH, D), b_dw.reshape(1, D), gn.reshape(1, D),
      w_pw2.astype(BF16), b_pw2.reshape(1, D))


def kernel(x_prompt, x_sample, norm_g, final_g, ffn1_wg, ffn1_wu, ffn1_wd, ffn2_wg, ffn2_wu, ffn2_wd,
           mla_wq_a, mla_gq, mla_wq_b, mla_wkv_a, mla_gkv, mla_wkv_b, mla_wo,
           fnet_wo, fnet_bo, sgu_w_in, sgu_gv, sgu_ws, sgu_bs, sgu_w_out,
           conv_w_pw1, conv_b_pw1, conv_w_dw, conv_b_dw, conv_gn, conv_w_pw2, conv_b_pw2):
    bp, sp, D = x_prompt.shape
    bs_, ss, _ = x_sample.shape
    assert D == D_MODEL
    n_p, n_s = bp * sp, bs_ * ss
    groups = ((0, bp, sp), (n_p, bs_, ss))
    x = jnp.concatenate([x_prompt.reshape(n_p, D), x_sample.reshape(n_s, D)], axis=0)
    tm = 512
    depth = norm_g.shape[0]
    assert depth == 4

    def ffn_w(wg, wu, wd, i):
        return wg[i].astype(BF16), wu[i].astype(BF16), wd[i].astype(BF16)

    x = _ffn(x, norm_g[0, 0], *ffn_w(ffn1_wg, ffn1_wu, ffn1_wd, 0), tm=tm)
    o = _mla(x, groups, norm_g[0, 1], mla_wq_a[0], mla_gq[0], mla_wq_b[0], mla_wkv_a[0], mla_gkv[0],
             mla_wkv_b[0], tm=tm, tq=512, tk=512)
    x = _ffn(x, norm_g[0, 2], *ffn_w(ffn2_wg, ffn2_wu, ffn2_wd, 0), tm=tm, pre=(o, mla_wo[0].astype(BF16)))
    x = _ffn(x, norm_g[1, 0], *ffn_w(ffn1_wg, ffn1_wu, ffn1_wd, 1), tm=tm)
    x = _fnet(x, groups, norm_g[1, 1], fnet_wo[0], fnet_bo[0])
    x = _ffn(x, norm_g[1, 2], *ffn_w(ffn2_wg, ffn2_wu, ffn2_wd, 1), tm=tm)
    x = _ffn(x, norm_g[2, 0], *ffn_w(ffn1_wg, ffn1_wu, ffn1_wd, 2), tm=tm)
    x = _sgu(x, norm_g[2, 1], sgu_w_in[0], sgu_gv[0], sgu_ws[0], sgu_bs[0], sgu_w_out[0], tm=tm)
    x = _ffn(x, norm_g[2, 2], *ffn_w(ffn2_wg, ffn2_wu, ffn2_wd, 2), tm=tm)
    x = _ffn(x, norm_g[3, 0], *ffn_w(ffn1_wg, ffn1_wu, ffn1_wd, 3), tm=tm)
    x = _conv(x, groups, norm_g[3, 1], conv_w_pw1[0], conv_b_pw1[0], conv_w_dw[0], conv_b_dw[0],
              conv_gn[0], conv_w_pw2[0], conv_b_pw2[0], tm=tm)
    w3 = ffn_w(ffn2_wg, ffn2_wu, ffn2_wd, 3)
    y_p = _ffn(x, norm_g[3, 2], *w3, tm=tm, row_start=0, n_rows=n_p, final_g=final_g)
    y_s = _ffn(x, norm_g[3, 2], *w3, tm=tm, row_start=n_p, n_rows=n_s, final_g=final_g)
    return y_p.reshape(bp, sp, D), y_s.reshape(bs_, ss, D)
```

```python
import functools
import math

import numpy as np
import jax
import jax.numpy as jnp
from jax import lax
from jax.experimental import pallas as pl
from jax.experimental.pallas import tpu as pltpu

F32 = jnp.float32
BF16 = jnp.bfloat16

EPS = 1e-6
D_MODEL = 1024
D_FF = 2816
FF_CHUNKS = (1024, 1024, 768)
HEADS = 16
QK_NOPE = 64
QK_ROPE = 32
V_HEAD = 64
Q_LORA = 384
KV_LORA = 256
ROPE_THETA = 10000.0
ATTN_SCALE = 1.0 / math.sqrt(QK_NOPE + QK_ROPE)
HEAD_PAD = 128
MAX_HI_COL = QK_NOPE + QK_ROPE
MAX_LO_COL = MAX_HI_COL + 1
KEY_UNIT = 256
LOG2E = math.log2(math.e)
FNET_GROUPS = 4
FNET_GC = D_MODEL // FNET_GROUPS
SGU_CHUNK = 128
SGU_HALF = 2 * D_MODEL
SGU_GROUPS = 4
SGU_GC = SGU_HALF // SGU_GROUPS
CONV_WIDTH = 31
CONV_PAD = (CONV_WIDTH - 1) // 2
HALO = 16
CONV_COLS = 256
SUBLANES = 8
LANES = 128
MIB = 1 << 20


def _params(n_axes, vmem_mib):
    return pltpu.CompilerParams(
        dimension_semantics=("arbitrary",) * n_axes,
        vmem_limit_bytes=vmem_mib * MIB)


def _resident(shape):
    nd = len(shape)
    return pl.BlockSpec(shape, lambda *_: (0,) * nd, pipeline_mode=pl.Buffered(1))


def _rms(x, g):
    return x * lax.rsqrt(jnp.mean(x * x, axis=-1, keepdims=True) + EPS) * g


def _dot(a, b):
    return jnp.dot(a, b, preferred_element_type=F32)


def _ffn_body(x, g_ref, wg_ref, wu_ref, wd_ref):
    h = _rms(x, g_ref[...]).astype(BF16)
    acc = None
    off = 0
    for c in FF_CHUNKS:
        gate = _dot(h, wg_ref[:, off:off + c])
        up = _dot(h, wu_ref[:, off:off + c])
        a = (gate * jax.nn.sigmoid(gate) * up).astype(BF16)
        part = _dot(a, wd_ref[off:off + c, :])
        acc = part if acc is None else acc + part
        off += c
    return x + 0.5 * acc


def _ffn_kernel(x_ref, g_ref, wg_ref, wu_ref, wd_ref, o_ref):
    o_ref[...] = _ffn_body(x_ref[...], g_ref, wg_ref, wu_ref, wd_ref)


def _ffn_final_kernel(x_ref, g_ref, wg_ref, wu_ref, wd_ref, fg_ref, o_ref):
    y = _ffn_body(x_ref[...], g_ref, wg_ref, wu_ref, wd_ref)
    o_ref[...] = _rms(y, fg_ref[...])


def _ffn_proj_kernel(x_ref, o_in_ref, wo_ref, g_ref, wg_ref, wu_ref, wd_ref, o_ref):
    x = x_ref[...] + _dot(o_in_ref[...], wo_ref[...])
    o_ref[...] = _ffn_body(x, g_ref, wg_ref, wu_ref, wd_ref)


def _ffn(x, g, wg, wu, wd, *, tm, row_start=0, n_rows=None, final_g=None, pre=None):
    T, D = x.shape
    n_rows = T if n_rows is None else n_rows
    assert n_rows % tm == 0 and row_start % tm == 0
    blk0 = row_start // tm
    row_spec = pl.BlockSpec((tm, D), lambda i: (blk0 + i, 0))
    w_specs = [_resident((1, D)), _resident(wg.shape), _resident(wu.shape), _resident(wd.shape)]
    args = [x]
    in_specs = [row_spec]
    if pre is not None:
        o_in, wo = pre
        args += [o_in, wo]
        in_specs += [pl.BlockSpec((tm, o_in.shape[1]), lambda i: (blk0 + i, 0)), _resident(wo.shape)]
        body = _ffn_proj_kernel
    else:
        body = _ffn_kernel
    args += [g.reshape(1, D), wg, wu, wd]
    in_specs += w_specs
    if final_g is not None:
        assert pre is None
        args.append(final_g.reshape(1, D))
        in_specs.append(_resident((1, D)))
        body = _ffn_final_kernel
    return pl.pallas_call(
        body,
        grid=(n_rows // tm,),
        in_specs=in_specs,
        out_specs=pl.BlockSpec((tm, D), lambda i: (i, 0)),
        out_shape=jax.ShapeDtypeStruct((n_rows, D), F32),
        compiler_params=_params(1, 52),
        name="ffn",
    )(*args)


def _mla_proj_kernel(x_ref, g_ref, wqa_ref, gq_ref, wq1_ref, wq2_ref, wkva_ref, gkv_ref,
                     wkr_ref, wk_ref, wv_ref, cq_ref, sq_ref, ck_ref, sk_ref, kone_ref,
                     q_ref, k_ref, v_ref):
    h = _rms(x_ref[...], g_ref[...]).astype(BF16)
    cq = _rms(_dot(h, wqa_ref[...]), gq_ref[...]).astype(BF16)
    ckv = _rms(_dot(h, wkva_ref[...]), gkv_ref[...]).astype(BF16)
    kr2 = _dot(h, wkr_ref[...])
    kr = kr2[:, :HEAD_PAD] * ck_ref[...] + kr2[:, HEAD_PAD:] * sk_ref[...] + kone_ref[...]
    kr = jnp.concatenate([kr, kr], axis=1)
    cq_t = jnp.concatenate([cq_ref[...]] * 2, axis=1)
    sq_t = jnp.concatenate([sq_ref[...]] * 2, axis=1)
    for hp in range(HEADS // 2):
        sl = slice(2 * hp * HEAD_PAD, 2 * (hp + 1) * HEAD_PAD)
        q2 = (_dot(cq, wq1_ref[:, sl]) * cq_t + _dot(cq, wq2_ref[:, sl]) * sq_t).astype(BF16)
        k2 = (_dot(ckv, wk_ref[:, sl]) + kr).astype(BF16)
        for e in range(2):
            q_ref[2 * hp + e] = q2[:, e * HEAD_PAD:(e + 1) * HEAD_PAD]
            k_ref[2 * hp + e] = k2[:, e * HEAD_PAD:(e + 1) * HEAD_PAD]
    v = _dot(ckv, wv_ref[...])
    for hp in range(HEADS // 2):
        v_ref[hp] = v[:, hp * LANES:(hp + 1) * LANES].astype(BF16)


def _qk(q, k_ref, head, start):
    ks = k_ref[head, pl.ds(start, KEY_UNIT), :]
    return lax.dot_general(q, ks, (((1,), (1,)), ((), ())), preferred_element_type=F32)


def _attn_kernel(q_ref, k_ref, v_ref, o_ref, *, seq, unroll):
    tq = q_ref.shape[1]
    n_iter = seq // (KEY_UNIT * unroll)
    lane = lax.broadcasted_iota(jnp.int32, (tq, LANES), 1)
    outs = []
    for j in range(2):
        q = q_ref[j]

        def max_body(c, mx, j=j, q=q):
            for u in range(unroll):
                start = pl.multiple_of((c * unroll + u) * KEY_UNIT, KEY_UNIT)
                s = _qk(q, k_ref, j, start)
                mx = jnp.maximum(mx, jnp.maximum(s[:, :LANES], s[:, LANES:]))
            return mx

        mx = lax.fori_loop(0, n_iter, max_body, jnp.full((tq, LANES), -jnp.inf, F32))
        m = jnp.max(mx, axis=-1, keepdims=True)
        m_hi = m.astype(BF16).astype(F32)
        m_lo = m - m_hi
        qa = jnp.where(lane == MAX_HI_COL, -m_hi, jnp.where(lane == MAX_LO_COL, -m_lo, q.astype(F32)))
        qa = qa.astype(BF16)

        def acc_body(c, carry, j=j, qa=qa):
            l, acc = carry
            for u in range(unroll):
                start = pl.multiple_of((c * unroll + u) * KEY_UNIT, KEY_UNIT)
                p = jnp.exp2(_qk(qa, k_ref, j, start))
                l = l + (p[:, :LANES] + p[:, LANES:])
                acc = acc + _dot(p.astype(BF16), v_ref[0, pl.ds(start, KEY_UNIT), :])
            return l, acc

        zeros = jnp.zeros((tq, LANES), F32)
        l, acc = lax.fori_loop(0, n_iter, acc_body, (zeros, zeros))
        outs.append(acc / jnp.sum(l, axis=-1, keepdims=True))
    o_ref[...] = jnp.where(lane < V_HEAD, outs[0], outs[1]).astype(o_ref.dtype)


def _rope_tables(seq_len):
    pos = jnp.arange(seq_len, dtype=F32)
    inv_freq = 1.0 / (ROPE_THETA ** (jnp.arange(0, QK_ROPE, 2, dtype=F32) / QK_ROPE))
    ang = pos[:, None] * inv_freq[None, :]
    cos, sin = jnp.cos(ang), jnp.sin(ang)
    ones = jnp.ones((seq_len, QK_NOPE), F32)
    zeros_n = jnp.zeros((seq_len, QK_NOPE), F32)
    zeros_p = jnp.zeros((seq_len, HEAD_PAD - QK_NOPE - QK_ROPE), F32)
    c_tab = jnp.concatenate([ones, cos, cos, zeros_p], axis=1)
    s_tab = jnp.concatenate([zeros_n, -sin, sin, zeros_p], axis=1)
    return c_tab, s_tab


def _swap_halves(w):
    half = w.shape[-1] // 2
    return jnp.concatenate([w[..., half:], w[..., :half]], axis=-1)


def _mla(x, groups, g, wq_a, gq, wq_b, wkv_a, gkv, wkv_b, *, tm, tq, unroll):
    T, D = x.shape
    pad = HEAD_PAD - QK_NOPE - QK_ROPE
    wq3 = wq_b.reshape(Q_LORA, HEADS, QK_NOPE + QK_ROPE)
    qn, qr = wq3[..., :QK_NOPE], wq3[..., QK_NOPE:]
    zp = jnp.zeros((Q_LORA, HEADS, pad), F32)
    wq1 = jnp.concatenate([qn, qr, zp], -1).reshape(Q_LORA, HEADS * HEAD_PAD).astype(BF16)
    wq2 = jnp.concatenate([jnp.zeros_like(qn), _swap_halves(qr), zp], -1).reshape(Q_LORA, HEADS * HEAD_PAD).astype(BF16)
    wkva_c = wkv_a[:, :KV_LORA].astype(BF16)
    wr = wkv_a[:, KV_LORA:]
    zn = jnp.zeros((D, QK_NOPE), F32)
    zpd = jnp.zeros((D, pad), F32)
    wkr = jnp.concatenate([zn, wr, zpd, zn, _swap_halves(wr), zpd], axis=1).astype(BF16)
    wkv3 = wkv_b.reshape(KV_LORA, HEADS, QK_NOPE + V_HEAD)
    wk = jnp.concatenate([wkv3[..., :QK_NOPE], jnp.zeros((KV_LORA, HEADS, HEAD_PAD - QK_NOPE), F32)], -1)
    wk = wk.reshape(KV_LORA, HEADS * HEAD_PAD).astype(BF16)
    wv = wkv3[..., QK_NOPE:].reshape(KV_LORA, HEADS * V_HEAD).astype(BF16)

    s_max = max(s for _, _, s in groups)
    c_tab, s_tab = _rope_tables(s_max)
    cq_tab, sq_tab = c_tab * (ATTN_SCALE * LOG2E), s_tab * (ATTN_SCALE * LOG2E)
    kone = jnp.zeros((1, HEAD_PAD), F32).at[0, MAX_HI_COL:MAX_LO_COL + 1].set(1.0)

    bounds = [(off // tm, (off + b * s) // tm, s // tm) for off, b, s in groups]

    def pos_block(i):
        blk = i
        for lo, _, per in bounds:
            blk = jnp.where(i >= lo, (i - lo) % per, blk)
        return (blk, 0)

    for off, b, s in groups:
        assert s % tm == 0 and off % tm == 0
    tab_spec = pl.BlockSpec((tm, HEAD_PAD), pos_block)
    q, k, v = pl.pallas_call(
        _mla_proj_kernel,
        grid=(T // tm,),
        in_specs=[pl.BlockSpec((tm, D), lambda i: (i, 0)), _resident((1, D)),
                  _resident(wq_a.shape), _resident((1, Q_LORA)), _resident(wq1.shape), _resident(wq2.shape),
                  _resident(wkva_c.shape), _resident((1, KV_LORA)), _resident(wkr.shape),
                  _resident(wk.shape), _resident(wv.shape),
                  tab_spec, tab_spec, tab_spec, tab_spec, _resident((1, HEAD_PAD))],
        out_specs=[pl.BlockSpec((HEADS, tm, HEAD_PAD), lambda i: (0, i, 0)),
                   pl.BlockSpec((HEADS, tm, HEAD_PAD), lambda i: (0, i, 0)),
                   pl.BlockSpec((HEADS // 2, tm, LANES), lambda i: (0, i, 0))],
        out_shape=[jax.ShapeDtypeStruct((HEADS, T, HEAD_PAD), BF16),
                   jax.ShapeDtypeStruct((HEADS, T, HEAD_PAD), BF16),
                   jax.ShapeDtypeStruct((HEADS // 2, T, LANES), BF16)],
        compiler_params=_params(1, 48),
        name="mla_proj",
    )(x, g.reshape(1, D), wq_a.astype(BF16), gq.reshape(1, Q_LORA), wq1, wq2,
      wkva_c, gkv.reshape(1, KV_LORA), wkr, wk, wv, cq_tab, sq_tab, c_tab, s_tab, kone)

    o = None
    for off, nb, s in groups:
        n_unroll = min(unroll, s // KEY_UNIT)
        assert off % s == 0 and s % tq == 0 and s % (KEY_UNIT * n_unroll) == 0
        seq0 = off // s
        nq = s // tq
        q0 = off // tq
        in_specs = [pl.BlockSpec((2, tq, HEAD_PAD), lambda b, hp, qi: (hp, q0 + b * nq + qi, 0)),
                    pl.BlockSpec((2, s, HEAD_PAD), lambda b, hp, qi: (hp, seq0 + b, 0)),
                    pl.BlockSpec((1, s, LANES), lambda b, hp, qi: (hp, seq0 + b, 0))]
        args = [q, k, v]
        aliases = {}
        if o is not None:
            in_specs.append(pl.BlockSpec(memory_space=pl.ANY))
            args.append(o)
            aliases = {3: 0}
        kern = functools.partial(_attn_kernel if o is None else _attn_kernel_aliased, seq=s, unroll=n_unroll)
        o = pl.pallas_call(
            kern,
            grid=(nb, HEADS // 2, nq),
            in_specs=in_specs,
            out_specs=pl.BlockSpec((tq, LANES), lambda b, hp, qi: (q0 + b * nq + qi, hp)),
            out_shape=jax.ShapeDtypeStruct((T, HEADS * V_HEAD), BF16),
            input_output_aliases=aliases,
            compiler_params=_params(3, 48),
            name="mla_attn",
        )(*args)
    return o


def _attn_kernel_aliased(q_ref, k_ref, v_ref, prev_ref, o_ref, *, seq, unroll):
    del prev_ref
    _attn_kernel(q_ref, k_ref, v_ref, o_ref, seq=seq, unroll=unroll)


def _proj_res_kernel(x_ref, a_ref, w_ref, o_ref):
    o_ref[...] = x_ref[...] + _dot(a_ref[...], w_ref[...])


def _proj_res(x, a, w, *, tm):
    T, D = x.shape
    return pl.pallas_call(
        _proj_res_kernel,
        grid=(T // tm,),
        in_specs=[pl.BlockSpec((tm, D), lambda i: (i, 0)),
                  pl.BlockSpec((tm, a.shape[1]), lambda i: (i, 0)), _resident(w.shape)],
        out_specs=pl.BlockSpec((tm, D), lambda i: (i, 0)),
        out_shape=jax.ShapeDtypeStruct((T, D), F32),
        compiler_params=_params(1, 40),
        name="proj_res",
    )(x, a, w)


def _dft_cos_sin(n):
    idx = np.arange(n)
    ang = 2.0 * np.pi * ((idx[:, None] * idx[None, :]) % n) / n
    scale = 1.0 / math.sqrt(n)
    return np.cos(ang) * scale, np.sin(ang) * scale


def _fnet_factor(s):
    na = 1 << (int(math.log2(s)) // 2)
    nb = s // na
    assert na * nb == s and nb % (2 * SUBLANES) == 0 and na % SUBLANES == 0
    return na, nb


def _fnet_stage1_kernel(x_ref, g_ref, cc_ref, nsc_ref, k1c_ref, k1s_ref, twc_ref, tws_ref,
                        ar_ref, ai_ref):
    na = x_ref.shape[0]
    rows = na * SUBLANES
    halves_r, halves_i = [], []
    for half in range(2):
        sub = slice(half * SUBLANES, (half + 1) * SUBLANES)
        xb = x_ref[:, sub, :].reshape(rows, D_MODEL)
        h = _rms(xb, g_ref[...]).astype(BF16)
        twc = twc_ref[:, sub, :].reshape(rows, LANES)
        tws = tws_ref[:, sub, :].reshape(rows, LANES)
        twc = jnp.concatenate([twc] * (FNET_GC // LANES), axis=1)
        tws = jnp.concatenate([tws] * (FNET_GC // LANES), axis=1)
        cols_r, cols_i = [], []
        for grp in range(FNET_GROUPS):
            hg = h[:, grp * FNET_GC:(grp + 1) * FNET_GC]
            zr = _dot(hg, cc_ref[...]).astype(BF16)
            zi = _dot(hg, nsc_ref[...]).astype(BF16)
            k1c = k1c_ref[...]
            k1s = k1s_ref[...]
            a_r = _dot(k1c, zr) + _dot(k1s, zi)
            a_i = _dot(k1c, zi) - _dot(k1s, zr)
            cols_r.append(a_r * twc + a_i * tws)
            cols_i.append(a_i * twc - a_r * tws)
        halves_r.append(jnp.concatenate(cols_r, axis=1).reshape(na, SUBLANES, D_MODEL))
        halves_i.append(jnp.concatenate(cols_i, axis=1).reshape(na, SUBLANES, D_MODEL))
    ar_ref[...] = jnp.concatenate(halves_r, axis=1).astype(BF16)
    ai_ref[...] = jnp.concatenate(halves_i, axis=1).astype(BF16)


def _fnet_stage2_kernel(x_ref, ar_ref, ai_ref, gc_ref, gs_ref, wo_ref, bo_ref, o_ref):
    nb = x_ref.shape[0]
    rows = nb * SUBLANES
    y = _dot(gc_ref[...], ar_ref[...]) + _dot(gs_ref[...], ai_ref[...])
    m = _dot(y.astype(BF16), wo_ref[...]) + bo_ref[...]
    o_ref[...] = (x_ref[...].reshape(rows, D_MODEL) + m).reshape(nb, SUBLANES, D_MODEL)


def _fnet(x, groups, g, wo, bo):
    T, D = x.shape
    cc, sc = _dft_cos_sin(FNET_GC)
    cc_j = jnp.asarray(cc, F32).astype(BF16)
    nsc_j = jnp.asarray(-sc, F32).astype(BF16)
    wo_b = wo.astype(BF16)
    for off, nbatch, s in groups:
        assert off % s == 0
        na, nb = _fnet_factor(s)
        seq0 = off // s
        c1, s1 = _dft_cos_sin(na)
        eye = np.eye(SUBLANES)
        k1c = jnp.asarray(np.kron(c1, eye), F32).astype(BF16)
        k1s = jnp.asarray(np.kron(s1, eye), F32).astype(BF16)
        c2, s2 = _dft_cos_sin(nb)
        gc = np.zeros((nb, SUBLANES, SUBLANES, nb))
        gs = np.zeros((nb, SUBLANES, SUBLANES, nb))
        for j in range(SUBLANES):
            gc[:, j, j, :] = c2
            gs[:, j, j, :] = s2
        n2 = nb * SUBLANES
        gc_j = jnp.asarray(gc.reshape(n2, n2), F32).astype(BF16)
        gs_j = jnp.asarray(gs.reshape(n2, n2), F32).astype(BF16)
        k1i = jnp.arange(na, dtype=jnp.int32)[:, None]
        bi = jnp.arange(nb, dtype=jnp.int32)[None, :]
        ang = ((k1i * bi) % s).astype(F32) * (2.0 * math.pi / s)
        twc = jnp.broadcast_to(jnp.cos(ang)[:, :, None], (na, nb, LANES))
        tws = jnp.broadcast_to(jnp.sin(ang)[:, :, None], (na, nb, LANES))

        tb = 2 * SUBLANES
        n1 = na * SUBLANES
        x3 = x.reshape(T // nb, nb, D)
        blk3 = pl.BlockSpec((na, tb, D), lambda b, j: (seq0 + b, j, 0))
        tw_spec = pl.BlockSpec((na, tb, LANES), lambda b, j: (0, j, 0))
        a_r, a_i = pl.pallas_call(
            _fnet_stage1_kernel,
            grid=(nbatch, nb // tb),
            in_specs=[blk3, _resident((1, D)), _resident(cc_j.shape), _resident(nsc_j.shape),
                      _resident((n1, n1)), _resident((n1, n1)), tw_spec, tw_spec],
            out_specs=[pl.BlockSpec((na, tb, D), lambda b, j: (b, j, 0))] * 2,
            out_shape=[jax.ShapeDtypeStruct((nbatch * na, nb, D), BF16)] * 2,
            compiler_params=_params(2, 56),
            name="fnet_stage1",
        )(x3, g.reshape(1, D), cc_j, nsc_j, k1c, k1s, twc, tws)

        a_r = a_r.reshape(nbatch * s, D)
        a_i = a_i.reshape(nbatch * s, D)
        xo = x.reshape(T // na, na, D)
        xblk = pl.BlockSpec((nb, SUBLANES, D), lambda b, j: (seq0 + b, j, 0))
        ablk = pl.BlockSpec((n2, D), lambda b, j: (b * (na // SUBLANES) + j, 0))
        xo = pl.pallas_call(
            _fnet_stage2_kernel,
            grid=(nbatch, na // SUBLANES),
            in_specs=[xblk, ablk, ablk, _resident((n2, n2)), _resident((n2, n2)),
                      _resident(wo_b.shape), _resident((1, D))],
            out_specs=xblk,
            out_shape=jax.ShapeDtypeStruct(xo.shape, F32),
            input_output_aliases={0: 0},
            compiler_params=_params(2, 56),
            name="fnet_stage2",
        )(xo, a_r, a_i, gc_j, gs_j, wo_b, bo.reshape(1, D))
        x = xo.reshape(T, D)
    return x


def _gelu_tanh(x):
    return 0.5 * x * (1.0 + jnp.tanh(math.sqrt(2.0 / math.pi) * (x + 0.044715 * (x * x * x))))


def _sgu_kernel(x_ref, g_ref, wu_ref, wv_ref, gv_ref, ws_ref, bs_ref, wout_ref, o_ref):
    x = x_ref[...]
    tm = x.shape[0]
    h = _rms(x, g_ref[...]).astype(BF16)
    v = _gelu_tanh(_dot(h, wv_ref[...]))
    v = _rms(v, gv_ref[...]).astype(BF16)
    u = _gelu_tanh(_dot(h, wu_ref[...]))
    rows = []
    for c in range(tm // SGU_CHUNK):
        r = slice(c * SGU_CHUNK, (c + 1) * SGU_CHUNK)
        cols = []
        for grp in range(SGU_GROUPS):
            cs = slice(grp * SGU_GC, (grp + 1) * SGU_GC)
            sv = _dot(ws_ref[grp], v[r, cs]) + bs_ref[grp]
            cols.append((u[r, cs] * sv).astype(BF16))
        rows.append(jnp.concatenate(cols, axis=1))
    gated = jnp.concatenate(rows, axis=0)
    o_ref[...] = x + _dot(gated, wout_ref[...])


def _sgu(x, g, w_in, gv, ws, bs, w_out, *, tm):
    T, D = x.shape
    assert tm % SGU_CHUNK == 0
    wu = w_in[:, :SGU_HALF].astype(BF16)
    wv = w_in[:, SGU_HALF:].astype(BF16)
    bs_b = jnp.broadcast_to(bs[:, :, None], (SGU_GROUPS, SGU_CHUNK, SGU_GC)).astype(F32)
    return pl.pallas_call(
        _sgu_kernel,
        grid=(T // tm,),
        in_specs=[pl.BlockSpec((tm, D), lambda i: (i, 0)), _resident((1, D)),
                  _resident(wu.shape), _resident(wv.shape), _resident((1, SGU_HALF)),
                  _resident(ws.shape), _resident(bs_b.shape), _resident(w_out.shape)],
        out_specs=pl.BlockSpec((tm, D), lambda i: (i, 0)),
        out_shape=jax.ShapeDtypeStruct((T, D), F32),
        compiler_params=_params(1, 52),
        name="sgu",
    )(x, g.reshape(1, D), wu, wv, gv.reshape(1, SGU_HALF), ws.astype(BF16), bs_b, w_out.astype(BF16))


def _conv_kernel(x_ref, xp_ref, xn_ref, g_ref, w1_ref, b1_ref, wdw_ref, bdw_ref, gn_ref, w2_ref, b2_ref,
                 o_ref, glu_ref, sh_ref, dw_ref, *, seq_starts, seq_ends):
    tm = x_ref.shape[0]
    i = pl.program_id(0)
    start = i * tm
    is_first = functools.reduce(jnp.logical_or, [start == s for s in seq_starts])
    is_last = functools.reduce(jnp.logical_or, [start + tm == e for e in seq_ends])
    x = x_ref[...]
    xa = jnp.concatenate([xp_ref[...], x, xn_ref[...]], axis=0)
    h = _rms(xa, g_ref[...]).astype(BF16)
    a = _dot(h, w1_ref[...]) + b1_ref[...]
    glu = a[:, :D_MODEL] * jax.nn.sigmoid(a[:, D_MODEL:])
    row = lax.broadcasted_iota(jnp.int32, (tm + 2 * HALO, 1), 0)
    outside = jnp.logical_or(jnp.logical_and(is_first, row < HALO),
                             jnp.logical_and(is_last, row >= tm + HALO))
    glu_ref[...] = jnp.where(outside, 0.0, glu)
    n_sh = tm + 2 * HALO - SUBLANES
    for cb in range(D_MODEL // CONV_COLS):
        cs = slice(cb * CONV_COLS, (cb + 1) * CONV_COLS)
        for b in range(SUBLANES):
            sh_ref[b] = glu_ref[pl.ds(b, n_sh), cs]
        acc = jnp.zeros((tm, CONV_COLS), F32) + bdw_ref[:, cs]
        for j in range(CONV_WIDTH):
            a8, b = divmod(HALO - CONV_PAD + j, SUBLANES)
            acc = acc + sh_ref[b, pl.ds(a8 * SUBLANES, tm), :] * wdw_ref[j:j + 1, cs]
        dw_ref[:, cs] = acc
    y = _rms(dw_ref[...], gn_ref[...])
    y = (y * jax.nn.sigmoid(y)).astype(BF16)
    o_ref[...] = x + _dot(y, w2_ref[...]) + b2_ref[...]


def _conv(x, groups, g, w_pw1, b_pw1, w_dw, b_dw, gn, w_pw2, b_pw2, *, tm):
    T, D = x.shape
    seq_starts = tuple(off + b * s for off, nb, s in groups for b in range(nb))
    seq_ends = tuple(off + (b + 1) * s for off, nb, s in groups for b in range(nb))
    for off, nb, s in groups:
        assert s % tm == 0 and off % tm == 0
    r = tm // HALO
    last_halo = T // HALO - 1
    kern = functools.partial(_conv_kernel, seq_starts=seq_starts, seq_ends=seq_ends)
    return pl.pallas_call(
        kern,
        grid=(T // tm,),
        in_specs=[pl.BlockSpec((tm, D), lambda i: (i, 0)),
                  pl.BlockSpec((HALO, D), lambda i: (jnp.maximum(i * r - 1, 0), 0)),
                  pl.BlockSpec((HALO, D), lambda i: (jnp.minimum((i + 1) * r, last_halo), 0)),
                  _resident((1, D)), _resident(w_pw1.shape), _resident((1, 2 * D)),
                  _resident((CONV_WIDTH, D)), _resident((1, D)), _resident((1, D)),
                  _resident(w_pw2.shape), _resident((1, D))],
        out_specs=pl.BlockSpec((tm, D), lambda i: (i, 0)),
        out_shape=jax.ShapeDtypeStruct((T, D), F32),
        scratch_shapes=[pltpu.VMEM((tm + 2 * HALO, D), F32),
                        pltpu.VMEM((SUBLANES, tm + 2 * HALO - SUBLANES, CONV_COLS), F32),
                        pltpu.VMEM((tm, D), F32)],
        compiler_params=_params(1, 48),
        name="conv",
    )(x, x, x, g.reshape(1, D), w_pw1.astype(BF16), b_pw1.reshape(1, 2 * D),
      w_dw.reshape(CONV_WIDTH, D), b_dw.reshape(1, D), gn.reshape(1, D),
      w_pw2.astype(BF16), b_pw2.reshape(1, D))


def kernel(x_prompt, x_sample, norm_g, final_g, ffn1_wg, ffn1_wu, ffn1_wd, ffn2_wg, ffn2_wu, ffn2_wd,
           mla_wq_a, mla_gq, mla_wq_b, mla_wkv_a, mla_gkv, mla_wkv_b, mla_wo,
           fnet_wo, fnet_bo, sgu_w_in, sgu_gv, sgu_ws, sgu_bs, sgu_w_out,
           conv_w_pw1, conv_b_pw1, conv_w_dw, conv_b_dw, conv_gn, conv_w_pw2, conv_b_pw2):
    bp, sp, D = x_prompt.shape
    bs_, ss, _ = x_sample.shape
    assert D == D_MODEL
    n_p, n_s = bp * sp, bs_ * ss
    groups = ((0, bp, sp), (n_p, bs_, ss))
    x = jnp.concatenate([x_prompt.reshape(n_p, D), x_sample.reshape(n_s, D)], axis=0)
    tm = 512
    depth = norm_g.shape[0]
    assert depth == 4

    def ffn_w(wg, wu, wd, i):
        return wg[i].astype(BF16), wu[i].astype(BF16), wd[i].astype(BF16)

    x = _ffn(x, norm_g[0, 0], *ffn_w(ffn1_wg, ffn1_wu, ffn1_wd, 0), tm=tm)
    o = _mla(x, groups, norm_g[0, 1], mla_wq_a[0], mla_gq[0], mla_wq_b[0], mla_wkv_a[0], mla_gkv[0],
             mla_wkv_b[0], tm=tm, tq=512, unroll=16)
    x = _ffn(x, norm_g[0, 2], *ffn_w(ffn2_wg, ffn2_wu, ffn2_wd, 0), tm=tm, pre=(o, mla_wo[0].astype(BF16)))
    x = _ffn(x, norm_g[1, 0], *ffn_w(ffn1_wg, ffn1_wu, ffn1_wd, 1), tm=tm)
    x = _fnet(x, groups, norm_g[1, 1], fnet_wo[0], fnet_bo[0])
    x = _ffn(x, norm_g[1, 2], *ffn_w(ffn2_wg, ffn2_wu, ffn2_wd, 1), tm=tm)
    x = _ffn(x, norm_g[2, 0], *ffn_w(ffn1_wg, ffn1_wu, ffn1_wd, 2), tm=tm)
    x = _sgu(x, norm_g[2, 1], sgu_w_in[0], sgu_gv[0], sgu_ws[0], sgu_bs[0], sgu_w_out[0], tm=tm)
    x = _ffn(x, norm_g[2, 2], *ffn_w(ffn2_wg, ffn2_wu, ffn2_wd, 2), tm=tm)
    x = _ffn(x, norm_g[3, 0], *ffn_w(ffn1_wg, ffn1_wu, ffn1_wd, 3), tm=tm)
    x = _conv(x, groups, norm_g[3, 1], conv_w_pw1[0], conv_b_pw1[0], conv_w_dw[0], conv_b_dw[0],
              conv_gn[0], conv_w_pw2[0], conv_b_pw2[0], tm=tm)
    w3 = ffn_w(ffn2_wg, ffn2_wu, ffn2_wd, 3)
    y_p = _ffn(x, norm_g[3, 2], *w3, tm=tm, row_start=0, n_rows=n_p, final_g=final_g)
    y_s = _ffn(x, norm_g[3, 2], *w3, tm=tm, row_start=n_p, n_rows=n_s, final_g=final_g)
    return y_p.reshape(bp, sp, D), y_s.reshape(bs_, ss, D)
```

```python
import functools
import math

import numpy as np
import jax
import jax.numpy as jnp
from jax import lax
from jax.experimental import pallas as pl
from jax.experimental.pallas import tpu as pltpu

F32 = jnp.float32
BF16 = jnp.bfloat16

EPS = 1e-6
D_MODEL = 1024
D_FF = 2816
FF_CHUNKS = (1024, 1024, 768)
FFN_TM = 1024
FFN_VMEM_MIB = 56
HEADS = 16
QK_NOPE = 64
QK_ROPE = 32
V_HEAD = 64
Q_LORA = 384
KV_LORA = 256
ROPE_THETA = 10000.0
ATTN_SCALE = 1.0 / math.sqrt(QK_NOPE + QK_ROPE)
HEAD_PAD = 128
MAX_HI_COL = QK_NOPE + QK_ROPE
MAX_LO_COL = MAX_HI_COL + 1
KEY_UNIT = 256
LOG2E = math.log2(math.e)
NORM_ROWS = 512
SAFE_BOUND_LOG2 = 30.0 * LOG2E
MLA_TILES = dict(tm=512, tq=1024, unroll=16)
FNET_GROUPS = 4
FNET_GC = D_MODEL // FNET_GROUPS
SGU_CHUNK = 128
SGU_HALF = 2 * D_MODEL
SGU_GROUPS = 4
SGU_GC = SGU_HALF // SGU_GROUPS
CONV_WIDTH = 31
CONV_PAD = (CONV_WIDTH - 1) // 2
HALO = 16
CONV_COLS = 256
SUBLANES = 8
LANES = 128
MIB = 1 << 20


def _params(n_axes, vmem_mib):
    return pltpu.CompilerParams(
        dimension_semantics=("arbitrary",) * n_axes,
        vmem_limit_bytes=vmem_mib * MIB)


def _resident(shape):
    nd = len(shape)
    return pl.BlockSpec(shape, lambda *_: (0,) * nd, pipeline_mode=pl.Buffered(1))


def _rms(x, g):
    return x * lax.rsqrt(jnp.mean(x * x, axis=-1, keepdims=True) + EPS) * g


def _dot(a, b):
    return jnp.dot(a, b, preferred_element_type=F32)


def _ffn_body(x, g_ref, wg_ref, wu_ref, wd_ref):
    h = _rms(x, g_ref[...]).astype(BF16)
    acc = None
    off = 0
    for c in FF_CHUNKS:
        gate = _dot(h, wg_ref[:, off:off + c])
        up = _dot(h, wu_ref[:, off:off + c])
        a = (gate * jax.nn.sigmoid(gate) * up).astype(BF16)
        part = _dot(a, wd_ref[off:off + c, :])
        acc = part if acc is None else acc + part
        off += c
    return x + 0.5 * acc


def _ffn_kernel(*refs, has_pre, has_final, has_prev):
    refs = list(refs)
    x = refs.pop(0)[...]
    if has_pre:
        o_in_ref, wo_ref = refs.pop(0), refs.pop(0)
        x = x + _dot(o_in_ref[...], wo_ref[...])
    g_ref, wg_ref, wu_ref, wd_ref = (refs.pop(0) for _ in range(4))
    fg_ref = refs.pop(0) if has_final else None
    if has_prev:
        refs.pop(0)
    (o_ref,) = refs
    y = _ffn_body(x, g_ref, wg_ref, wu_ref, wd_ref)
    o_ref[...] = _rms(y, fg_ref[...]) if has_final else y


def _ffn(x, g, wg, wu, wd, *, tm, row_start=0, n_rows=None, final_g=None, pre=None,
         out_rows=None, out_start=0, prev=None):
    T, D = x.shape
    n_rows = T if n_rows is None else n_rows
    out_rows = n_rows if out_rows is None else out_rows
    assert n_rows % tm == 0 and row_start % tm == 0 and out_start % tm == 0
    blk0, oblk0 = row_start // tm, out_start // tm
    args = [x]
    in_specs = [pl.BlockSpec((tm, D), lambda i: (blk0 + i, 0))]
    if pre is not None:
        o_in, wo = pre
        args += [o_in, wo]
        in_specs += [pl.BlockSpec((tm, o_in.shape[1]), lambda i: (blk0 + i, 0)), _resident(wo.shape)]
    args += [g.reshape(1, D), wg, wu, wd]
    in_specs += [_resident((1, D)), _resident(wg.shape), _resident(wu.shape), _resident(wd.shape)]
    if final_g is not None:
        args.append(final_g.reshape(1, D))
        in_specs.append(_resident((1, D)))
    aliases = {}
    if prev is not None:
        assert prev.shape == (out_rows, D)
        aliases = {len(args): 0}
        args.append(prev)
        in_specs.append(pl.BlockSpec(memory_space=pl.ANY))
    body = functools.partial(_ffn_kernel, has_pre=pre is not None, has_final=final_g is not None,
                             has_prev=prev is not None)
    return pl.pallas_call(
        body,
        grid=(n_rows // tm,),
        in_specs=in_specs,
        out_specs=pl.BlockSpec((tm, D), lambda i: (oblk0 + i, 0)),
        out_shape=jax.ShapeDtypeStruct((out_rows, D), F32),
        input_output_aliases=aliases,
        compiler_params=_params(1, FFN_VMEM_MIB),
        name="ffn",
    )(*args)


def _mla_proj_kernel(x_ref, g_ref, wqa_ref, gq_ref, wq1_ref, wq2_ref, wkva_ref, gkv_ref,
                     wkr_ref, wk_ref, wv_ref, cq_ref, sq_ref, ck_ref, sk_ref, kone_ref,
                     q_ref, k_ref, v_ref):
    h = _rms(x_ref[...], g_ref[...]).astype(BF16)
    cq = _rms(_dot(h, wqa_ref[...]), gq_ref[...]).astype(BF16)
    ckv = _rms(_dot(h, wkva_ref[...]), gkv_ref[...]).astype(BF16)
    kr2 = _dot(h, wkr_ref[...])
    kr = kr2[:, :HEAD_PAD] * ck_ref[...] + kr2[:, HEAD_PAD:] * sk_ref[...] + kone_ref[...]
    kr = jnp.concatenate([kr, kr], axis=1)
    cq_t = jnp.concatenate([cq_ref[...]] * 2, axis=1)
    sq_t = jnp.concatenate([sq_ref[...]] * 2, axis=1)
    for hp in range(HEADS // 2):
        sl = slice(2 * hp * HEAD_PAD, 2 * (hp + 1) * HEAD_PAD)
        q2 = (_dot(cq, wq1_ref[:, sl]) * cq_t + _dot(cq, wq2_ref[:, sl]) * sq_t).astype(BF16)
        k2 = (_dot(ckv, wk_ref[:, sl]) + kr).astype(BF16)
        for e in range(2):
            q_ref[2 * hp + e] = q2[:, e * HEAD_PAD:(e + 1) * HEAD_PAD]
            k_ref[2 * hp + e] = k2[:, e * HEAD_PAD:(e + 1) * HEAD_PAD]
    v = _dot(ckv, wv_ref[...])
    for hp in range(HEADS // 2):
        v_ref[hp] = v[:, hp * LANES:(hp + 1) * LANES].astype(BF16)


def _qk(q, k_ref, head, start):
    ks = k_ref[head, pl.ds(start, KEY_UNIT), :]
    return lax.dot_general(q, ks, (((1,), (1,)), ((), ())), preferred_element_type=F32)


def _attn_kernel(q_ref, k_ref, v_ref, o_ref, knorm_ref, shift_ref, *, seq, unroll):
    tq = q_ref.shape[1]
    n_iter = seq // (KEY_UNIT * unroll)
    lane = lax.broadcasted_iota(jnp.int32, (tq, LANES), 1)

    @pl.when(pl.program_id(2) == 0)
    def _key_norms():
        for j in range(2):
            def body(c, mx, j=j):
                start = pl.multiple_of(c * NORM_ROWS, NORM_ROWS)
                kk = k_ref[j, pl.ds(start, NORM_ROWS), :].astype(F32)
                return jnp.maximum(mx, jnp.sum(kk * kk, axis=-1, keepdims=True))

            mx = lax.fori_loop(0, seq // NORM_ROWS, body, jnp.zeros((NORM_ROWS, 1), F32))
            knorm_ref[j] = jnp.broadcast_to(jnp.max(mx, axis=0, keepdims=True), (SUBLANES, LANES))

    outs = []
    for j in range(2):
        q = q_ref[j]
        qf = q.astype(F32)
        bound = jnp.sqrt(jnp.sum(qf * qf, axis=-1, keepdims=True) * knorm_ref[j][0:1, 0:1])
        shift_ref[...] = jnp.broadcast_to(bound, (tq, LANES))

        @pl.when(jnp.max(bound) > SAFE_BOUND_LOG2)
        def _exact_max(j=j, q=q):
            def max_body(c, mx):
                for u in range(unroll):
                    start = pl.multiple_of((c * unroll + u) * KEY_UNIT, KEY_UNIT)
                    s = _qk(q, k_ref, j, start)
                    mx = jnp.maximum(mx, jnp.maximum(s[:, :LANES], s[:, LANES:]))
                return mx

            mx = lax.fori_loop(0, n_iter, max_body, jnp.full((tq, LANES), -jnp.inf, F32))
            shift_ref[...] = jnp.broadcast_to(jnp.max(mx, axis=-1, keepdims=True), (tq, LANES))

        m = shift_ref[...]
        m_hi = m.astype(BF16).astype(F32)
        m_lo = m - m_hi
        qa = jnp.where(lane == MAX_HI_COL, -m_hi, jnp.where(lane == MAX_LO_COL, -m_lo, q.astype(F32)))
        qa = qa.astype(BF16)

        def acc_body(c, carry, j=j, qa=qa):
            l, acc = carry
            for u in range(unroll):
                start = pl.multiple_of((c * unroll + u) * KEY_UNIT, KEY_UNIT)
                p = jnp.exp2(_qk(qa, k_ref, j, start))
                l = l + (p[:, :LANES] + p[:, LANES:])
                acc = acc + _dot(p.astype(BF16), v_ref[0, pl.ds(start, KEY_UNIT), :])
            return l, acc

        zeros = jnp.zeros((tq, LANES), F32)
        l, acc = lax.fori_loop(0, n_iter, acc_body, (zeros, zeros))
        outs.append(acc / jnp.sum(l, axis=-1, keepdims=True))
    o_ref[...] = jnp.where(lane < V_HEAD, outs[0], outs[1]).astype(o_ref.dtype)


def _rope_tables(seq_len):
    pos = jnp.arange(seq_len, dtype=F32)
    inv_freq = 1.0 / (ROPE_THETA ** (jnp.arange(0, QK_ROPE, 2, dtype=F32) / QK_ROPE))
    ang = pos[:, None] * inv_freq[None, :]
    cos, sin = jnp.cos(ang), jnp.sin(ang)
    ones = jnp.ones((seq_len, QK_NOPE), F32)
    zeros_n = jnp.zeros((seq_len, QK_NOPE), F32)
    zeros_p = jnp.zeros((seq_len, HEAD_PAD - QK_NOPE - QK_ROPE), F32)
    c_tab = jnp.concatenate([ones, cos, cos, zeros_p], axis=1)
    s_tab = jnp.concatenate([zeros_n, -sin, sin, zeros_p], axis=1)
    return c_tab, s_tab


def _swap_halves(w):
    half = w.shape[-1] // 2
    return jnp.concatenate([w[..., half:], w[..., :half]], axis=-1)


def _mla(x, groups, g, wq_a, gq, wq_b, wkv_a, gkv, wkv_b, *, tm, tq, unroll):
    T, D = x.shape
    pad = HEAD_PAD - QK_NOPE - QK_ROPE
    wq3 = wq_b.reshape(Q_LORA, HEADS, QK_NOPE + QK_ROPE)
    qn, qr = wq3[..., :QK_NOPE], wq3[..., QK_NOPE:]
    zp = jnp.zeros((Q_LORA, HEADS, pad), F32)
    wq1 = jnp.concatenate([qn, qr, zp], -1).reshape(Q_LORA, HEADS * HEAD_PAD).astype(BF16)
    wq2 = jnp.concatenate([jnp.zeros_like(qn), _swap_halves(qr), zp], -1).reshape(Q_LORA, HEADS * HEAD_PAD).astype(BF16)
    wkva_c = wkv_a[:, :KV_LORA].astype(BF16)
    wr = wkv_a[:, KV_LORA:]
    zn = jnp.zeros((D, QK_NOPE), F32)
    zpd = jnp.zeros((D, pad), F32)
    wkr = jnp.concatenate([zn, wr, zpd, zn, _swap_halves(wr), zpd], axis=1).astype(BF16)
    wkv3 = wkv_b.reshape(KV_LORA, HEADS, QK_NOPE + V_HEAD)
    wk = jnp.concatenate([wkv3[..., :QK_NOPE], jnp.zeros((KV_LORA, HEADS, HEAD_PAD - QK_NOPE), F32)], -1)
    wk = wk.reshape(KV_LORA, HEADS * HEAD_PAD).astype(BF16)
    wv = wkv3[..., QK_NOPE:].reshape(KV_LORA, HEADS * V_HEAD).astype(BF16)

    s_max = max(s for _, _, s in groups)
    c_tab, s_tab = _rope_tables(s_max)
    cq_tab, sq_tab = c_tab * (ATTN_SCALE * LOG2E), s_tab * (ATTN_SCALE * LOG2E)
    kone = jnp.zeros((1, HEAD_PAD), F32).at[0, MAX_HI_COL:MAX_LO_COL + 1].set(1.0)

    bounds = [(off // tm, (off + b * s) // tm, s // tm) for off, b, s in groups]

    def pos_block(i):
        blk = i
        for lo, _, per in bounds:
            blk = jnp.where(i >= lo, (i - lo) % per, blk)
        return (blk, 0)

    for off, b, s in groups:
        assert s % tm == 0 and off % tm == 0
    tab_spec = pl.BlockSpec((tm, HEAD_PAD), pos_block)
    q, k, v = pl.pallas_call(
        _mla_proj_kernel,
        grid=(T // tm,),
        in_specs=[pl.BlockSpec((tm, D), lambda i: (i, 0)), _resident((1, D)),
                  _resident(wq_a.shape), _resident((1, Q_LORA)), _resident(wq1.shape), _resident(wq2.shape),
                  _resident(wkva_c.shape), _resident((1, KV_LORA)), _resident(wkr.shape),
                  _resident(wk.shape), _resident(wv.shape),
                  tab_spec, tab_spec, tab_spec, tab_spec, _resident((1, HEAD_PAD))],
        out_specs=[pl.BlockSpec((HEADS, tm, HEAD_PAD), lambda i: (0, i, 0)),
                   pl.BlockSpec((HEADS, tm, HEAD_PAD), lambda i: (0, i, 0)),
                   pl.BlockSpec((HEADS // 2, tm, LANES), lambda i: (0, i, 0))],
        out_shape=[jax.ShapeDtypeStruct((HEADS, T, HEAD_PAD), BF16),
                   jax.ShapeDtypeStruct((HEADS, T, HEAD_PAD), BF16),
                   jax.ShapeDtypeStruct((HEADS // 2, T, LANES), BF16)],
        compiler_params=_params(1, 48),
        name="mla_proj",
    )(x, g.reshape(1, D), wq_a.astype(BF16), gq.reshape(1, Q_LORA), wq1, wq2,
      wkva_c, gkv.reshape(1, KV_LORA), wkr, wk, wv, cq_tab, sq_tab, c_tab, s_tab, kone)

    o = None
    for off, nb, s in groups:
        n_unroll = min(unroll, s // KEY_UNIT)
        assert off % s == 0 and s % tq == 0 and s % (KEY_UNIT * n_unroll) == 0
        seq0 = off // s
        nq = s // tq
        q0 = off // tq
        in_specs = [pl.BlockSpec((2, tq, HEAD_PAD), lambda b, hp, qi: (hp, q0 + b * nq + qi, 0)),
                    pl.BlockSpec((2, s, HEAD_PAD), lambda b, hp, qi: (hp, seq0 + b, 0)),
                    pl.BlockSpec((1, s, LANES), lambda b, hp, qi: (hp, seq0 + b, 0))]
        args = [q, k, v]
        aliases = {}
        if o is not None:
            in_specs.append(pl.BlockSpec(memory_space=pl.ANY))
            args.append(o)
            aliases = {3: 0}
        kern = functools.partial(_attn_kernel if o is None else _attn_kernel_aliased, seq=s, unroll=n_unroll)
        o = pl.pallas_call(
            kern,
            grid=(nb, HEADS // 2, nq),
            in_specs=in_specs,
            out_specs=pl.BlockSpec((tq, LANES), lambda b, hp, qi: (q0 + b * nq + qi, hp)),
            out_shape=jax.ShapeDtypeStruct((T, HEADS * V_HEAD), BF16),
            input_output_aliases=aliases,
            scratch_shapes=[pltpu.VMEM((2, SUBLANES, LANES), F32), pltpu.VMEM((tq, LANES), F32)],
            compiler_params=_params(3, 48),
            name="mla_attn",
        )(*args)
    return o


def _attn_kernel_aliased(q_ref, k_ref, v_ref, prev_ref, o_ref, knorm_ref, shift_ref, *, seq, unroll):
    del prev_ref
    _attn_kernel(q_ref, k_ref, v_ref, o_ref, knorm_ref, shift_ref, seq=seq, unroll=unroll)


def _proj_res_kernel(x_ref, a_ref, w_ref, o_ref):
    o_ref[...] = x_ref[...] + _dot(a_ref[...], w_ref[...])


def _proj_res(x, a, w, *, tm):
    T, D = x.shape
    return pl.pallas_call(
        _proj_res_kernel,
        grid=(T // tm,),
        in_specs=[pl.BlockSpec((tm, D), lambda i: (i, 0)),
                  pl.BlockSpec((tm, a.shape[1]), lambda i: (i, 0)), _resident(w.shape)],
        out_specs=pl.BlockSpec((tm, D), lambda i: (i, 0)),
        out_shape=jax.ShapeDtypeStruct((T, D), F32),
        compiler_params=_params(1, 40),
        name="proj_res",
    )(x, a, w)


def _dft_cos_sin(n):
    idx = np.arange(n)
    ang = 2.0 * np.pi * ((idx[:, None] * idx[None, :]) % n) / n
    scale = 1.0 / math.sqrt(n)
    return np.cos(ang) * scale, np.sin(ang) * scale


def _fnet_factor(s):
    na = 1 << (int(math.log2(s)) // 2)
    nb = s // na
    assert na * nb == s and nb % (2 * SUBLANES) == 0 and na % SUBLANES == 0
    return na, nb


def _fnet_stage1_kernel(x_ref, g_ref, cc_ref, nsc_ref, k1c_ref, k1s_ref, twc_ref, tws_ref,
                        ar_ref, ai_ref):
    na = x_ref.shape[0]
    rows = na * SUBLANES
    halves_r, halves_i = [], []
    for half in range(2):
        sub = slice(half * SUBLANES, (half + 1) * SUBLANES)
        xb = x_ref[:, sub, :].reshape(rows, D_MODEL)
        h = _rms(xb, g_ref[...]).astype(BF16)
        twc = twc_ref[:, sub, :].reshape(rows, LANES)
        tws = tws_ref[:, sub, :].reshape(rows, LANES)
        twc = jnp.concatenate([twc] * (FNET_GC // LANES), axis=1)
        tws = jnp.concatenate([tws] * (FNET_GC // LANES), axis=1)
        cols_r, cols_i = [], []
        for grp in range(FNET_GROUPS):
            hg = h[:, grp * FNET_GC:(grp + 1) * FNET_GC]
            zr = _dot(hg, cc_ref[...]).astype(BF16)
            zi = _dot(hg, nsc_ref[...]).astype(BF16)
            k1c = k1c_ref[...]
            k1s = k1s_ref[...]
            a_r = _dot(k1c, zr) + _dot(k1s, zi)
            a_i = _dot(k1c, zi) - _dot(k1s, zr)
            cols_r.append(a_r * twc + a_i * tws)
            cols_i.append(a_i * twc - a_r * tws)
        halves_r.append(jnp.concatenate(cols_r, axis=1).reshape(na, SUBLANES, D_MODEL))
        halves_i.append(jnp.concatenate(cols_i, axis=1).reshape(na, SUBLANES, D_MODEL))
    ar_ref[...] = jnp.concatenate(halves_r, axis=1).astype(BF16)
    ai_ref[...] = jnp.concatenate(halves_i, axis=1).astype(BF16)


def _fnet_stage2_kernel(x_ref, ar_ref, ai_ref, gc_ref, gs_ref, wo_ref, bo_ref, o_ref):
    nb = x_ref.shape[0]
    rows = nb * SUBLANES
    y = _dot(gc_ref[...], ar_ref[...]) + _dot(gs_ref[...], ai_ref[...])
    m = _dot(y.astype(BF16), wo_ref[...]) + bo_ref[...]
    o_ref[...] = (x_ref[...].reshape(rows, D_MODEL) + m).reshape(nb, SUBLANES, D_MODEL)


def _fnet(x, groups, g, wo, bo):
    T, D = x.shape
    cc, sc = _dft_cos_sin(FNET_GC)
    cc_j = jnp.asarray(cc, F32).astype(BF16)
    nsc_j = jnp.asarray(-sc, F32).astype(BF16)
    wo_b = wo.astype(BF16)
    for off, nbatch, s in groups:
        assert off % s == 0
        na, nb = _fnet_factor(s)
        seq0 = off // s
        c1, s1 = _dft_cos_sin(na)
        eye = np.eye(SUBLANES)
        k1c = jnp.asarray(np.kron(c1, eye), F32).astype(BF16)
        k1s = jnp.asarray(np.kron(s1, eye), F32).astype(BF16)
        c2, s2 = _dft_cos_sin(nb)
        gc = np.zeros((nb, SUBLANES, SUBLANES, nb))
        gs = np.zeros((nb, SUBLANES, SUBLANES, nb))
        for j in range(SUBLANES):
            gc[:, j, j, :] = c2
            gs[:, j, j, :] = s2
        n2 = nb * SUBLANES
        gc_j = jnp.asarray(gc.reshape(n2, n2), F32).astype(BF16)
        gs_j = jnp.asarray(gs.reshape(n2, n2), F32).astype(BF16)
        k1i = jnp.arange(na, dtype=jnp.int32)[:, None]
        bi = jnp.arange(nb, dtype=jnp.int32)[None, :]
        ang = ((k1i * bi) % s).astype(F32) * (2.0 * math.pi / s)
        twc = jnp.broadcast_to(jnp.cos(ang)[:, :, None], (na, nb, LANES))
        tws = jnp.broadcast_to(jnp.sin(ang)[:, :, None], (na, nb, LANES))

        tb = 2 * SUBLANES
        n1 = na * SUBLANES
        x3 = x.reshape(T // nb, nb, D)
        blk3 = pl.BlockSpec((na, tb, D), lambda b, j: (seq0 + b, j, 0))
        tw_spec = pl.BlockSpec((na, tb, LANES), lambda b, j: (0, j, 0))
        a_r, a_i = pl.pallas_call(
            _fnet_stage1_kernel,
            grid=(nbatch, nb // tb),
            in_specs=[blk3, _resident((1, D)), _resident(cc_j.shape), _resident(nsc_j.shape),
                      _resident((n1, n1)), _resident((n1, n1)), tw_spec, tw_spec],
            out_specs=[pl.BlockSpec((na, tb, D), lambda b, j: (b, j, 0))] * 2,
            out_shape=[jax.ShapeDtypeStruct((nbatch * na, nb, D), BF16)] * 2,
            compiler_params=_params(2, 56),
            name="fnet_stage1",
        )(x3, g.reshape(1, D), cc_j, nsc_j, k1c, k1s, twc, tws)

        a_r = a_r.reshape(nbatch * s, D)
        a_i = a_i.reshape(nbatch * s, D)
        xo = x.reshape(T // na, na, D)
        xblk = pl.BlockSpec((nb, SUBLANES, D), lambda b, j: (seq0 + b, j, 0))
        ablk = pl.BlockSpec((n2, D), lambda b, j: (b * (na // SUBLANES) + j, 0))
        xo = pl.pallas_call(
            _fnet_stage2_kernel,
            grid=(nbatch, na // SUBLANES),
            in_specs=[xblk, ablk, ablk, _resident((n2, n2)), _resident((n2, n2)),
                      _resident(wo_b.shape), _resident((1, D))],
            out_specs=xblk,
            out_shape=jax.ShapeDtypeStruct(xo.shape, F32),
            input_output_aliases={0: 0},
            compiler_params=_params(2, 56),
            name="fnet_stage2",
        )(xo, a_r, a_i, gc_j, gs_j, wo_b, bo.reshape(1, D))
        x = xo.reshape(T, D)
    return x


def _gelu_tanh(x):
    return 0.5 * x * (1.0 + jnp.tanh(math.sqrt(2.0 / math.pi) * (x + 0.044715 * (x * x * x))))


def _sgu_kernel(x_ref, g_ref, wu_ref, wv_ref, gv_ref, ws_ref, bs_ref, wout_ref, o_ref):
    x = x_ref[...]
    tm = x.shape[0]
    h = _rms(x, g_ref[...]).astype(BF16)
    v = _gelu_tanh(_dot(h, wv_ref[...]))
    v = _rms(v, gv_ref[...]).astype(BF16)
    u = _gelu_tanh(_dot(h, wu_ref[...]))
    rows = []
    for c in range(tm // SGU_CHUNK):
        r = slice(c * SGU_CHUNK, (c + 1) * SGU_CHUNK)
        cols = []
        for grp in range(SGU_GROUPS):
            cs = slice(grp * SGU_GC, (grp + 1) * SGU_GC)
            sv = _dot(ws_ref[grp], v[r, cs]) + bs_ref[grp]
            cols.append((u[r, cs] * sv).astype(BF16))
        rows.append(jnp.concatenate(cols, axis=1))
    gated = jnp.concatenate(rows, axis=0)
    o_ref[...] = x + _dot(gated, wout_ref[...])


def _sgu(x, g, w_in, gv, ws, bs, w_out, *, tm):
    T, D = x.shape
    assert tm % SGU_CHUNK == 0
    wu = w_in[:, :SGU_HALF].astype(BF16)
    wv = w_in[:, SGU_HALF:].astype(BF16)
    bs_b = jnp.broadcast_to(bs[:, :, None], (SGU_GROUPS, SGU_CHUNK, SGU_GC)).astype(F32)
    return pl.pallas_call(
        _sgu_kernel,
        grid=(T // tm,),
        in_specs=[pl.BlockSpec((tm, D), lambda i: (i, 0)), _resident((1, D)),
                  _resident(wu.shape), _resident(wv.shape), _resident((1, SGU_HALF)),
                  _resident(ws.shape), _resident(bs_b.shape), _resident(w_out.shape)],
        out_specs=pl.BlockSpec((tm, D), lambda i: (i, 0)),
        out_shape=jax.ShapeDtypeStruct((T, D), F32),
        compiler_params=_params(1, 52),
        name="sgu",
    )(x, g.reshape(1, D), wu, wv, gv.reshape(1, SGU_HALF), ws.astype(BF16), bs_b, w_out.astype(BF16))


def _conv_kernel(x_ref, xp_ref, xn_ref, g_ref, w1_ref, b1_ref, wdw_ref, bdw_ref, gn_ref, w2_ref, b2_ref,
                 o_ref, glu_ref, sh_ref, dw_ref, *, seq_starts, seq_ends):
    tm = x_ref.shape[0]
    i = pl.program_id(0)
    start = i * tm
    is_first = functools.reduce(jnp.logical_or, [start == s for s in seq_starts])
    is_last = functools.reduce(jnp.logical_or, [start + tm == e for e in seq_ends])
    x = x_ref[...]
    xa = jnp.concatenate([xp_ref[...], x, xn_ref[...]], axis=0)
    h = _rms(xa, g_ref[...]).astype(BF16)
    a = _dot(h, w1_ref[...]) + b1_ref[...]
    glu = a[:, :D_MODEL] * jax.nn.sigmoid(a[:, D_MODEL:])
    row = lax.broadcasted_iota(jnp.int32, (tm + 2 * HALO, 1), 0)
    outside = jnp.logical_or(jnp.logical_and(is_first, row < HALO),
                             jnp.logical_and(is_last, row >= tm + HALO))
    glu_ref[...] = jnp.where(outside, 0.0, glu)
    n_sh = tm + 2 * HALO - SUBLANES
    for cb in range(D_MODEL // CONV_COLS):
        cs = slice(cb * CONV_COLS, (cb + 1) * CONV_COLS)
        for b in range(SUBLANES):
            sh_ref[b] = glu_ref[pl.ds(b, n_sh), cs]
        acc = jnp.zeros((tm, CONV_COLS), F32) + bdw_ref[:, cs]
        for j in range(CONV_WIDTH):
            a8, b = divmod(HALO - CONV_PAD + j, SUBLANES)
            acc = acc + sh_ref[b, pl.ds(a8 * SUBLANES, tm), :] * wdw_ref[j:j + 1, cs]
        dw_ref[:, cs] = acc
    y = _rms(dw_ref[...], gn_ref[...])
    y = (y * jax.nn.sigmoid(y)).astype(BF16)
    o_ref[...] = x + _dot(y, w2_ref[...]) + b2_ref[...]


def _conv(x, groups, g, w_pw1, b_pw1, w_dw, b_dw, gn, w_pw2, b_pw2, *, tm):
    T, D = x.shape
    seq_starts = tuple(off + b * s for off, nb, s in groups for b in range(nb))
    seq_ends = tuple(off + (b + 1) * s for off, nb, s in groups for b in range(nb))
    for off, nb, s in groups:
        assert s % tm == 0 and off % tm == 0
    r = tm // HALO
    last_halo = T // HALO - 1
    kern = functools.partial(_conv_kernel, seq_starts=seq_starts, seq_ends=seq_ends)
    return pl.pallas_call(
        kern,
        grid=(T // tm,),
        in_specs=[pl.BlockSpec((tm, D), lambda i: (i, 0)),
                  pl.BlockSpec((HALO, D), lambda i: (jnp.maximum(i * r - 1, 0), 0)),
                  pl.BlockSpec((HALO, D), lambda i: (jnp.minimum((i + 1) * r, last_halo), 0)),
                  _resident((1, D)), _resident(w_pw1.shape), _resident((1, 2 * D)),
                  _resident((CONV_WIDTH, D)), _resident((1, D)), _resident((1, D)),
                  _resident(w_pw2.shape), _resident((1, D))],
        out_specs=pl.BlockSpec((tm, D), lambda i: (i, 0)),
        out_shape=jax.ShapeDtypeStruct((T, D), F32),
        scratch_shapes=[pltpu.VMEM((tm + 2 * HALO, D), F32),
                        pltpu.VMEM((SUBLANES, tm + 2 * HALO - SUBLANES, CONV_COLS), F32),
                        pltpu.VMEM((tm, D), F32)],
        compiler_params=_params(1, 48),
        name="conv",
    )(x, x, x, g.reshape(1, D), w_pw1.astype(BF16), b_pw1.reshape(1, 2 * D),
      w_dw.reshape(CONV_WIDTH, D), b_dw.reshape(1, D), gn.reshape(1, D),
      w_pw2.astype(BF16), b_pw2.reshape(1, D))


def kernel(x_prompt, x_sample, norm_g, final_g, ffn1_wg, ffn1_wu, ffn1_wd, ffn2_wg, ffn2_wu, ffn2_wd,
           mla_wq_a, mla_gq, mla_wq_b, mla_wkv_a, mla_gkv, mla_wkv_b, mla_wo,
           fnet_wo, fnet_bo, sgu_w_in, sgu_gv, sgu_ws, sgu_bs, sgu_w_out,
           conv_w_pw1, conv_b_pw1, conv_w_dw, conv_b_dw, conv_gn, conv_w_pw2, conv_b_pw2):
    bp, sp, D = x_prompt.shape
    bs_, ss, _ = x_sample.shape
    assert D == D_MODEL
    n_p, n_s = bp * sp, bs_ * ss
    groups = ((0, bp, sp), (n_p, bs_, ss))
    n_tok = n_p + n_s
    tm = 512
    depth = norm_g.shape[0]
    assert depth == 4

    def ffn_w(wg, wu, wd, i):
        return wg[i].astype(BF16), wu[i].astype(BF16), wd[i].astype(BF16)

    def ffn(x, *a, **kw):
        return _ffn(x, *a, tm=FFN_TM, **kw)

    w0 = ffn_w(ffn1_wg, ffn1_wu, ffn1_wd, 0)
    x = ffn(x_prompt.reshape(n_p, D), norm_g[0, 0], *w0, out_rows=n_tok)
    x = ffn(x_sample.reshape(n_s, D), norm_g[0, 0], *w0, out_rows=n_tok, out_start=n_p, prev=x)
    o = _mla(x, groups, norm_g[0, 1], mla_wq_a[0], mla_gq[0], mla_wq_b[0], mla_wkv_a[0], mla_gkv[0],
             mla_wkv_b[0], **MLA_TILES)
    x = ffn(x, norm_g[0, 2], *ffn_w(ffn2_wg, ffn2_wu, ffn2_wd, 0), pre=(o, mla_wo[0].astype(BF16)))
    x = ffn(x, norm_g[1, 0], *ffn_w(ffn1_wg, ffn1_wu, ffn1_wd, 1))
    x = _fnet(x, groups, norm_g[1, 1], fnet_wo[0], fnet_bo[0])
    x = ffn(x, norm_g[1, 2], *ffn_w(ffn2_wg, ffn2_wu, ffn2_wd, 1))
    x = ffn(x, norm_g[2, 0], *ffn_w(ffn1_wg, ffn1_wu, ffn1_wd, 2))
    x = _sgu(x, norm_g[2, 1], sgu_w_in[0], sgu_gv[0], sgu_ws[0], sgu_bs[0], sgu_w_out[0], tm=tm)
    x = ffn(x, norm_g[2, 2], *ffn_w(ffn2_wg, ffn2_wu, ffn2_wd, 2))
    x = ffn(x, norm_g[3, 0], *ffn_w(ffn1_wg, ffn1_wu, ffn1_wd, 3))
    x = _conv(x, groups, norm_g[3, 1], conv_w_pw1[0], conv_b_pw1[0], conv_w_dw[0], conv_b_dw[0],
              conv_gn[0], conv_w_pw2[0], conv_b_pw2[0], tm=tm)
    w3 = ffn_w(ffn2_wg, ffn2_wu, ffn2_wd, 3)
    y_p = ffn(x, norm_g[3, 2], *w3, row_start=0, n_rows=n_p, final_g=final_g)
    y_s = ffn(x, norm_g[3, 2], *w3, row_start=n_p, n_rows=n_s, final_g=final_g)
    return y_p.reshape(bp, sp, D), y_s.reshape(bs_, ss, D)
```

```python
import functools
import math

import numpy as np
import jax
import jax.numpy as jnp
from jax import lax
from jax.experimental import pallas as pl
from jax.experimental.pallas import tpu as pltpu

F32 = jnp.float32
BF16 = jnp.bfloat16

EPS = 1e-6
D_MODEL = 1024
D_FF = 2816
FF_CHUNKS = (1024, 1024, 768)
FFN_TM = 1024
FFN_VMEM_MIB = 56
HEADS = 16
QK_NOPE = 64
QK_ROPE = 32
V_HEAD = 64
Q_LORA = 384
KV_LORA = 256
ROPE_THETA = 10000.0
ATTN_SCALE = 1.0 / math.sqrt(QK_NOPE + QK_ROPE)
HEAD_PAD = 128
MAX_HI_COL = QK_NOPE + QK_ROPE
MAX_LO_COL = MAX_HI_COL + 1
KEY_UNIT = 256
LOG2E = math.log2(math.e)
NORM_ROWS = 2048
SAFE_BOUND_LOG2 = 30.0 * LOG2E
MLA_TILES = dict(tm=512, tq=1024, unroll=16)
FNET_GROUPS = 4
FNET_GC = D_MODEL // FNET_GROUPS
SGU_CHUNK = 128
SGU_HALF = 2 * D_MODEL
SGU_GROUPS = 4
SGU_GC = SGU_HALF // SGU_GROUPS
CONV_WIDTH = 31
CONV_PAD = (CONV_WIDTH - 1) // 2
HALO = 16
CONV_COLS = 256
SUBLANES = 8
LANES = 128
MIB = 1 << 20


def _params(n_axes, vmem_mib):
    return pltpu.CompilerParams(
        dimension_semantics=("arbitrary",) * n_axes,
        vmem_limit_bytes=vmem_mib * MIB)


def _resident(shape):
    nd = len(shape)
    return pl.BlockSpec(shape, lambda *_: (0,) * nd, pipeline_mode=pl.Buffered(1))


def _rms(x, g):
    return x * lax.rsqrt(jnp.mean(x * x, axis=-1, keepdims=True) + EPS) * g


def _dot(a, b):
    return jnp.dot(a, b, preferred_element_type=F32)


def _ffn_body(x, g_ref, wg_ref, wu_ref, wd_ref):
    h = _rms(x, g_ref[...]).astype(BF16)
    acc = None
    off = 0
    for c in FF_CHUNKS:
        gate = _dot(h, wg_ref[:, off:off + c])
        up = _dot(h, wu_ref[:, off:off + c])
        a = (gate * jax.nn.sigmoid(gate) * up).astype(BF16)
        part = _dot(a, wd_ref[off:off + c, :])
        acc = part if acc is None else acc + part
        off += c
    return x + 0.5 * acc


def _ffn_kernel(*refs, has_pre, has_final, has_prev):
    refs = list(refs)
    x = refs.pop(0)[...]
    if has_pre:
        o_in_ref, wo_ref = refs.pop(0), refs.pop(0)
        x = x + _dot(o_in_ref[...], wo_ref[...])
    g_ref, wg_ref, wu_ref, wd_ref = (refs.pop(0) for _ in range(4))
    fg_ref = refs.pop(0) if has_final else None
    if has_prev:
        refs.pop(0)
    (o_ref,) = refs
    y = _ffn_body(x, g_ref, wg_ref, wu_ref, wd_ref)
    o_ref[...] = _rms(y, fg_ref[...]) if has_final else y


def _ffn(x, g, wg, wu, wd, *, tm, row_start=0, n_rows=None, final_g=None, pre=None,
         out_rows=None, out_start=0, prev=None):
    T, D = x.shape
    n_rows = T if n_rows is None else n_rows
    out_rows = n_rows if out_rows is None else out_rows
    assert n_rows % tm == 0 and row_start % tm == 0 and out_start % tm == 0
    blk0, oblk0 = row_start // tm, out_start // tm
    args = [x]
    in_specs = [pl.BlockSpec((tm, D), lambda i: (blk0 + i, 0))]
    if pre is not None:
        o_in, wo = pre
        args += [o_in, wo]
        in_specs += [pl.BlockSpec((tm, o_in.shape[1]), lambda i: (blk0 + i, 0)), _resident(wo.shape)]
    args += [g.reshape(1, D), wg, wu, wd]
    in_specs += [_resident((1, D)), _resident(wg.shape), _resident(wu.shape), _resident(wd.shape)]
    if final_g is not None:
        args.append(final_g.reshape(1, D))
        in_specs.append(_resident((1, D)))
    aliases = {}
    if prev is not None:
        assert prev.shape == (out_rows, D)
        aliases = {len(args): 0}
        args.append(prev)
        in_specs.append(pl.BlockSpec(memory_space=pl.ANY))
    body = functools.partial(_ffn_kernel, has_pre=pre is not None, has_final=final_g is not None,
                             has_prev=prev is not None)
    return pl.pallas_call(
        body,
        grid=(n_rows // tm,),
        in_specs=in_specs,
        out_specs=pl.BlockSpec((tm, D), lambda i: (oblk0 + i, 0)),
        out_shape=jax.ShapeDtypeStruct((out_rows, D), F32),
        input_output_aliases=aliases,
        compiler_params=_params(1, FFN_VMEM_MIB),
        name="ffn",
    )(*args)


def _mla_proj_kernel(x_ref, g_ref, wqa_ref, gq_ref, wq1_ref, wq2_ref, wkva_ref, gkv_ref,
                     wkr_ref, wk_ref, wv_ref, cq_ref, sq_ref, ck_ref, sk_ref, kone_ref,
                     q_ref, k_ref, v_ref):
    h = _rms(x_ref[...], g_ref[...]).astype(BF16)
    cq = _rms(_dot(h, wqa_ref[...]), gq_ref[...]).astype(BF16)
    ckv = _rms(_dot(h, wkva_ref[...]), gkv_ref[...]).astype(BF16)
    kr2 = _dot(h, wkr_ref[...])
    kr = kr2[:, :HEAD_PAD] * ck_ref[...] + kr2[:, HEAD_PAD:] * sk_ref[...] + kone_ref[...]
    kr = jnp.concatenate([kr, kr], axis=1)
    cq_t = jnp.concatenate([cq_ref[...]] * 2, axis=1)
    sq_t = jnp.concatenate([sq_ref[...]] * 2, axis=1)
    for hp in range(HEADS // 2):
        sl = slice(2 * hp * HEAD_PAD, 2 * (hp + 1) * HEAD_PAD)
        q2 = (_dot(cq, wq1_ref[:, sl]) * cq_t + _dot(cq, wq2_ref[:, sl]) * sq_t).astype(BF16)
        k2 = (_dot(ckv, wk_ref[:, sl]) + kr).astype(BF16)
        for e in range(2):
            q_ref[2 * hp + e] = q2[:, e * HEAD_PAD:(e + 1) * HEAD_PAD]
            k_ref[2 * hp + e] = k2[:, e * HEAD_PAD:(e + 1) * HEAD_PAD]
    v = _dot(ckv, wv_ref[...])
    for hp in range(HEADS // 2):
        v_ref[hp] = v[:, hp * LANES:(hp + 1) * LANES].astype(BF16)


def _row_sums(x, terms):
    ones = jnp.ones((LANES, LANES), BF16)
    total = None
    for _ in range(terms):
        piece = x.astype(BF16)
        part = _dot(piece, ones)
        total = part if total is None else total + part
        x = x - piece.astype(F32)
    return total


def _qk(q, k_ref, head, start):
    ks = k_ref[head, pl.ds(start, KEY_UNIT), :]
    return lax.dot_general(q, ks, (((1,), (1,)), ((), ())), preferred_element_type=F32)


def _attn_kernel(q_ref, k_ref, v_ref, o_ref, knorm_ref, shift_ref, *, seq, unroll):
    tq = q_ref.shape[1]
    n_iter = seq // (KEY_UNIT * unroll)
    lane = lax.broadcasted_iota(jnp.int32, (tq, LANES), 1)

    @pl.when(pl.program_id(2) == 0)
    def _key_norms():
        for j in range(2):
            def body(c, mx, j=j):
                start = pl.multiple_of(c * NORM_ROWS, NORM_ROWS)
                kk = k_ref[j, pl.ds(start, NORM_ROWS), :].astype(F32)
                return jnp.maximum(mx, _row_sums(kk * kk, 2))

            mx = lax.fori_loop(0, seq // NORM_ROWS, body, jnp.zeros((NORM_ROWS, LANES), F32))
            knorm_ref[j] = jnp.broadcast_to(jnp.max(mx, axis=0, keepdims=True), (SUBLANES, LANES))

    outs = []
    for j in range(2):
        q = q_ref[j]
        qf = q.astype(F32)
        bound = jnp.sqrt(_row_sums(qf * qf, 2) * knorm_ref[j][0:1, :])
        shift_ref[...] = bound

        @pl.when(jnp.max(bound) > SAFE_BOUND_LOG2)
        def _exact_max(j=j, q=q):
            def max_body(c, mx):
                for u in range(unroll):
                    start = pl.multiple_of((c * unroll + u) * KEY_UNIT, KEY_UNIT)
                    s = _qk(q, k_ref, j, start)
                    mx = jnp.maximum(mx, jnp.maximum(s[:, :LANES], s[:, LANES:]))
                return mx

            mx = lax.fori_loop(0, n_iter, max_body, jnp.full((tq, LANES), -jnp.inf, F32))
            shift_ref[...] = jnp.broadcast_to(jnp.max(mx, axis=-1, keepdims=True), (tq, LANES))

        m = shift_ref[...]
        m_hi = m.astype(BF16).astype(F32)
        m_lo = m - m_hi
        qa = jnp.where(lane == MAX_HI_COL, -m_hi, jnp.where(lane == MAX_LO_COL, -m_lo, q.astype(F32)))
        qa = qa.astype(BF16)

        def acc_body(c, carry, j=j, qa=qa):
            l, acc = carry
            for u in range(unroll):
                start = pl.multiple_of((c * unroll + u) * KEY_UNIT, KEY_UNIT)
                p = jnp.exp2(_qk(qa, k_ref, j, start))
                l = l + (p[:, :LANES] + p[:, LANES:])
                acc = acc + _dot(p.astype(BF16), v_ref[0, pl.ds(start, KEY_UNIT), :])
            return l, acc

        zeros = jnp.zeros((tq, LANES), F32)
        l, acc = lax.fori_loop(0, n_iter, acc_body, (zeros, zeros))
        outs.append(acc / _row_sums(l, 3))
    o_ref[...] = jnp.where(lane < V_HEAD, outs[0], outs[1]).astype(o_ref.dtype)


def _rope_tables(seq_len):
    pos = jnp.arange(seq_len, dtype=F32)
    inv_freq = 1.0 / (ROPE_THETA ** (jnp.arange(0, QK_ROPE, 2, dtype=F32) / QK_ROPE))
    ang = pos[:, None] * inv_freq[None, :]
    cos, sin = jnp.cos(ang), jnp.sin(ang)
    ones = jnp.ones((seq_len, QK_NOPE), F32)
    zeros_n = jnp.zeros((seq_len, QK_NOPE), F32)
    zeros_p = jnp.zeros((seq_len, HEAD_PAD - QK_NOPE - QK_ROPE), F32)
    c_tab = jnp.concatenate([ones, cos, cos, zeros_p], axis=1)
    s_tab = jnp.concatenate([zeros_n, -sin, sin, zeros_p], axis=1)
    return c_tab, s_tab


def _swap_halves(w):
    half = w.shape[-1] // 2
    return jnp.concatenate([w[..., half:], w[..., :half]], axis=-1)


def _mla(x, groups, g, wq_a, gq, wq_b, wkv_a, gkv, wkv_b, *, tm, tq, unroll):
    T, D = x.shape
    pad = HEAD_PAD - QK_NOPE - QK_ROPE
    wq3 = wq_b.reshape(Q_LORA, HEADS, QK_NOPE + QK_ROPE)
    qn, qr = wq3[..., :QK_NOPE], wq3[..., QK_NOPE:]
    zp = jnp.zeros((Q_LORA, HEADS, pad), F32)
    wq1 = jnp.concatenate([qn, qr, zp], -1).reshape(Q_LORA, HEADS * HEAD_PAD).astype(BF16)
    wq2 = jnp.concatenate([jnp.zeros_like(qn), _swap_halves(qr), zp], -1).reshape(Q_LORA, HEADS * HEAD_PAD).astype(BF16)
    wkva_c = wkv_a[:, :KV_LORA].astype(BF16)
    wr = wkv_a[:, KV_LORA:]
    zn = jnp.zeros((D, QK_NOPE), F32)
    zpd = jnp.zeros((D, pad), F32)
    wkr = jnp.concatenate([zn, wr, zpd, zn, _swap_halves(wr), zpd], axis=1).astype(BF16)
    wkv3 = wkv_b.reshape(KV_LORA, HEADS, QK_NOPE + V_HEAD)
    wk = jnp.concatenate([wkv3[..., :QK_NOPE], jnp.zeros((KV_LORA, HEADS, HEAD_PAD - QK_NOPE), F32)], -1)
    wk = wk.reshape(KV_LORA, HEADS * HEAD_PAD).astype(BF16)
    wv = wkv3[..., QK_NOPE:].reshape(KV_LORA, HEADS * V_HEAD).astype(BF16)

    s_max = max(s for _, _, s in groups)
    c_tab, s_tab = _rope_tables(s_max)
    cq_tab, sq_tab = c_tab * (ATTN_SCALE * LOG2E), s_tab * (ATTN_SCALE * LOG2E)
    kone = jnp.zeros((1, HEAD_PAD), F32).at[0, MAX_HI_COL:MAX_LO_COL + 1].set(1.0)

    bounds = [(off // tm, (off + b * s) // tm, s // tm) for off, b, s in groups]

    def pos_block(i):
        blk = i
        for lo, _, per in bounds:
            blk = jnp.where(i >= lo, (i - lo) % per, blk)
        return (blk, 0)

    for off, b, s in groups:
        assert s % tm == 0 and off % tm == 0
    tab_spec = pl.BlockSpec((tm, HEAD_PAD), pos_block)
    q, k, v = pl.pallas_call(
        _mla_proj_kernel,
        grid=(T // tm,),
        in_specs=[pl.BlockSpec((tm, D), lambda i: (i, 0)), _resident((1, D)),
                  _resident(wq_a.shape), _resident((1, Q_LORA)), _resident(wq1.shape), _resident(wq2.shape),
                  _resident(wkva_c.shape), _resident((1, KV_LORA)), _resident(wkr.shape),
                  _resident(wk.shape), _resident(wv.shape),
                  tab_spec, tab_spec, tab_spec, tab_spec, _resident((1, HEAD_PAD))],
        out_specs=[pl.BlockSpec((HEADS, tm, HEAD_PAD), lambda i: (0, i, 0)),
                   pl.BlockSpec((HEADS, tm, HEAD_PAD), lambda i: (0, i, 0)),
                   pl.BlockSpec((HEADS // 2, tm, LANES), lambda i: (0, i, 0))],
        out_shape=[jax.ShapeDtypeStruct((HEADS, T, HEAD_PAD), BF16),
                   jax.ShapeDtypeStruct((HEADS, T, HEAD_PAD), BF16),
                   jax.ShapeDtypeStruct((HEADS // 2, T, LANES), BF16)],
        compiler_params=_params(1, 48),
        name="mla_proj",
    )(x, g.reshape(1, D), wq_a.astype(BF16), gq.reshape(1, Q_LORA), wq1, wq2,
      wkva_c, gkv.reshape(1, KV_LORA), wkr, wk, wv, cq_tab, sq_tab, c_tab, s_tab, kone)

    o = None
    for off, nb, s in groups:
        n_unroll = min(unroll, s // KEY_UNIT)
        assert off % s == 0 and s % tq == 0 and s % (KEY_UNIT * n_unroll) == 0
        seq0 = off // s
        nq = s // tq
        q0 = off // tq
        in_specs = [pl.BlockSpec((2, tq, HEAD_PAD), lambda b, hp, qi: (hp, q0 + b * nq + qi, 0)),
                    pl.BlockSpec((2, s, HEAD_PAD), lambda b, hp, qi: (hp, seq0 + b, 0)),
                    pl.BlockSpec((1, s, LANES), lambda b, hp, qi: (hp, seq0 + b, 0))]
        args = [q, k, v]
        aliases = {}
        if o is not None:
            in_specs.append(pl.BlockSpec(memory_space=pl.ANY))
            args.append(o)
            aliases = {3: 0}
        kern = functools.partial(_attn_kernel if o is None else _attn_kernel_aliased, seq=s, unroll=n_unroll)
        o = pl.pallas_call(
            kern,
            grid=(nb, HEADS // 2, nq),
            in_specs=in_specs,
            out_specs=pl.BlockSpec((tq, LANES), lambda b, hp, qi: (q0 + b * nq + qi, hp)),
            out_shape=jax.ShapeDtypeStruct((T, HEADS * V_HEAD), BF16),
            input_output_aliases=aliases,
            scratch_shapes=[pltpu.VMEM((2, SUBLANES, LANES), F32), pltpu.VMEM((tq, LANES), F32)],
            compiler_params=_params(3, 48),
            name="mla_attn",
        )(*args)
    return o


def _attn_kernel_aliased(q_ref, k_ref, v_ref, prev_ref, o_ref, knorm_ref, shift_ref, *, seq, unroll):
    del prev_ref
    _attn_kernel(q_ref, k_ref, v_ref, o_ref, knorm_ref, shift_ref, seq=seq, unroll=unroll)


def _proj_res_kernel(x_ref, a_ref, w_ref, o_ref):
    o_ref[...] = x_ref[...] + _dot(a_ref[...], w_ref[...])


def _proj_res(x, a, w, *, tm):
    T, D = x.shape
    return pl.pallas_call(
        _proj_res_kernel,
        grid=(T // tm,),
        in_specs=[pl.BlockSpec((tm, D), lambda i: (i, 0)),
                  pl.BlockSpec((tm, a.shape[1]), lambda i: (i, 0)), _resident(w.shape)],
        out_specs=pl.BlockSpec((tm, D), lambda i: (i, 0)),
        out_shape=jax.ShapeDtypeStruct((T, D), F32),
        compiler_params=_params(1, 40),
        name="proj_res",
    )(x, a, w)


def _dft_cos_sin(n):
    idx = np.arange(n)
    ang = 2.0 * np.pi * ((idx[:, None] * idx[None, :]) % n) / n
    scale = 1.0 / math.sqrt(n)
    return np.cos(ang) * scale, np.sin(ang) * scale


def _fnet_factor(s):
    na = 1 << (int(math.log2(s)) // 2)
    nb = s // na
    assert na * nb == s and nb % (2 * SUBLANES) == 0 and na % SUBLANES == 0
    return na, nb


def _fnet_stage1_kernel(x_ref, g_ref, cc_ref, nsc_ref, k1c_ref, k1s_ref, twc_ref, tws_ref,
                        ar_ref, ai_ref):
    na = x_ref.shape[0]
    rows = na * SUBLANES
    halves_r, halves_i = [], []
    for half in range(2):
        sub = slice(half * SUBLANES, (half + 1) * SUBLANES)
        xb = x_ref[:, sub, :].reshape(rows, D_MODEL)
        h = _rms(xb, g_ref[...]).astype(BF16)
        twc = twc_ref[:, sub, :].reshape(rows, LANES)
        tws = tws_ref[:, sub, :].reshape(rows, LANES)
        twc = jnp.concatenate([twc] * (FNET_GC // LANES), axis=1)
        tws = jnp.concatenate([tws] * (FNET_GC // LANES), axis=1)
        cols_r, cols_i = [], []
        for grp in range(FNET_GROUPS):
            hg = h[:, grp * FNET_GC:(grp + 1) * FNET_GC]
            zr = _dot(hg, cc_ref[...]).astype(BF16)
            zi = _dot(hg, nsc_ref[...]).astype(BF16)
            k1c = k1c_ref[...]
            k1s = k1s_ref[...]
            a_r = _dot(k1c, zr) + _dot(k1s, zi)
            a_i = _dot(k1c, zi) - _dot(k1s, zr)
            cols_r.append(a_r * twc + a_i * tws)
            cols_i.append(a_i * twc - a_r * tws)
        halves_r.append(jnp.concatenate(cols_r, axis=1).reshape(na, SUBLANES, D_MODEL))
        halves_i.append(jnp.concatenate(cols_i, axis=1).reshape(na, SUBLANES, D_MODEL))
    ar_ref[...] = jnp.concatenate(halves_r, axis=1).astype(BF16)
    ai_ref[...] = jnp.concatenate(halves_i, axis=1).astype(BF16)


def _fnet_stage2_kernel(x_ref, ar_ref, ai_ref, gc_ref, gs_ref, wo_ref, bo_ref, o_ref):
    nb = x_ref.shape[0]
    rows = nb * SUBLANES
    y = _dot(gc_ref[...], ar_ref[...]) + _dot(gs_ref[...], ai_ref[...])
    m = _dot(y.astype(BF16), wo_ref[...]) + bo_ref[...]
    o_ref[...] = (x_ref[...].reshape(rows, D_MODEL) + m).reshape(nb, SUBLANES, D_MODEL)


def _fnet(x, groups, g, wo, bo):
    T, D = x.shape
    cc, sc = _dft_cos_sin(FNET_GC)
    cc_j = jnp.asarray(cc, F32).astype(BF16)
    nsc_j = jnp.asarray(-sc, F32).astype(BF16)
    wo_b = wo.astype(BF16)
    for off, nbatch, s in groups:
        assert off % s == 0
        na, nb = _fnet_factor(s)
        seq0 = off // s
        c1, s1 = _dft_cos_sin(na)
        eye = np.eye(SUBLANES)
        k1c = jnp.asarray(np.kron(c1, eye), F32).astype(BF16)
        k1s = jnp.asarray(np.kron(s1, eye), F32).astype(BF16)
        c2, s2 = _dft_cos_sin(nb)
        gc = np.zeros((nb, SUBLANES, SUBLANES, nb))
        gs = np.zeros((nb, SUBLANES, SUBLANES, nb))
        for j in range(SUBLANES):
            gc[:, j, j, :] = c2
            gs[:, j, j, :] = s2
        n2 = nb * SUBLANES
        gc_j = jnp.asarray(gc.reshape(n2, n2), F32).astype(BF16)
        gs_j = jnp.asarray(gs.reshape(n2, n2), F32).astype(BF16)
        k1i = jnp.arange(na, dtype=jnp.int32)[:, None]
        bi = jnp.arange(nb, dtype=jnp.int32)[None, :]
        ang = ((k1i * bi) % s).astype(F32) * (2.0 * math.pi / s)
        twc = jnp.broadcast_to(jnp.cos(ang)[:, :, None], (na, nb, LANES))
        tws = jnp.broadcast_to(jnp.sin(ang)[:, :, None], (na, nb, LANES))

        tb = 2 * SUBLANES
        n1 = na * SUBLANES
        x3 = x.reshape(T // nb, nb, D)
        blk3 = pl.BlockSpec((na, tb, D), lambda b, j: (seq0 + b, j, 0))
        tw_spec = pl.BlockSpec((na, tb, LANES), lambda b, j: (0, j, 0))
        a_r, a_i = pl.pallas_call(
            _fnet_stage1_kernel,
            grid=(nbatch, nb // tb),
            in_specs=[blk3, _resident((1, D)), _resident(cc_j.shape), _resident(nsc_j.shape),
                      _resident((n1, n1)), _resident((n1, n1)), tw_spec, tw_spec],
            out_specs=[pl.BlockSpec((na, tb, D), lambda b, j: (b, j, 0))] * 2,
            out_shape=[jax.ShapeDtypeStruct((nbatch * na, nb, D), BF16)] * 2,
            compiler_params=_params(2, 56),
            name="fnet_stage1",
        )(x3, g.reshape(1, D), cc_j, nsc_j, k1c, k1s, twc, tws)

        a_r = a_r.reshape(nbatch * s, D)
        a_i = a_i.reshape(nbatch * s, D)
        xo = x.reshape(T // na, na, D)
        xblk = pl.BlockSpec((nb, SUBLANES, D), lambda b, j: (seq0 + b, j, 0))
        ablk = pl.BlockSpec((n2, D), lambda b, j: (b * (na // SUBLANES) + j, 0))
        xo = pl.pallas_call(
            _fnet_stage2_kernel,
            grid=(nbatch, na // SUBLANES),
            in_specs=[xblk, ablk, ablk, _resident((n2, n2)), _resident((n2, n2)),
                      _resident(wo_b.shape), _resident((1, D))],
            out_specs=xblk,
            out_shape=jax.ShapeDtypeStruct(xo.shape, F32),
            input_output_aliases={0: 0},
            compiler_params=_params(2, 56),
            name="fnet_stage2",
        )(xo, a_r, a_i, gc_j, gs_j, wo_b, bo.reshape(1, D))
        x = xo.reshape(T, D)
    return x


def _gelu_tanh(x):
    return 0.5 * x * (1.0 + jnp.tanh(math.sqrt(2.0 / math.pi) * (x + 0.044715 * (x * x * x))))


def _sgu_kernel(x_ref, g_ref, wu_ref, wv_ref, gv_ref, ws_ref, bs_ref, wout_ref, o_ref):
    x = x_ref[...]
    tm = x.shape[0]
    h = _rms(x, g_ref[...]).astype(BF16)
    v = _gelu_tanh(_dot(h, wv_ref[...]))
    v = _rms(v, gv_ref[...]).astype(BF16)
    u = _gelu_tanh(_dot(h, wu_ref[...]))
    rows = []
    for c in range(tm // SGU_CHUNK):
        r = slice(c * SGU_CHUNK, (c + 1) * SGU_CHUNK)
        cols = []
        for grp in range(SGU_GROUPS):
            cs = slice(grp * SGU_GC, (grp + 1) * SGU_GC)
            sv = _dot(ws_ref[grp], v[r, cs]) + bs_ref[grp]
            cols.append((u[r, cs] * sv).astype(BF16))
        rows.append(jnp.concatenate(cols, axis=1))
    gated = jnp.concatenate(rows, axis=0)
    o_ref[...] = x + _dot(gated, wout_ref[...])


def _sgu(x, g, w_in, gv, ws, bs, w_out, *, tm):
    T, D = x.shape
    assert tm % SGU_CHUNK == 0
    wu = w_in[:, :SGU_HALF].astype(BF16)
    wv = w_in[:, SGU_HALF:].astype(BF16)
    bs_b = jnp.broadcast_to(bs[:, :, None], (SGU_GROUPS, SGU_CHUNK, SGU_GC)).astype(F32)
    return pl.pallas_call(
        _sgu_kernel,
        grid=(T // tm,),
        in_specs=[pl.BlockSpec((tm, D), lambda i: (i, 0)), _resident((1, D)),
                  _resident(wu.shape), _resident(wv.shape), _resident((1, SGU_HALF)),
                  _resident(ws.shape), _resident(bs_b.shape), _resident(w_out.shape)],
        out_specs=pl.BlockSpec((tm, D), lambda i: (i, 0)),
        out_shape=jax.ShapeDtypeStruct((T, D), F32),
        compiler_params=_params(1, 52),
        name="sgu",
    )(x, g.reshape(1, D), wu, wv, gv.reshape(1, SGU_HALF), ws.astype(BF16), bs_b, w_out.astype(BF16))


def _conv_kernel(x_ref, xp_ref, xn_ref, g_ref, w1_ref, b1_ref, wdw_ref, bdw_ref, gn_ref, w2_ref, b2_ref,
                 o_ref, glu_ref, sh_ref, dw_ref, *, seq_starts, seq_ends):
    tm = x_ref.shape[0]
    i = pl.program_id(0)
    start = i * tm
    is_first = functools.reduce(jnp.logical_or, [start == s for s in seq_starts])
    is_last = functools.reduce(jnp.logical_or, [start + tm == e for e in seq_ends])
    x = x_ref[...]
    xa = jnp.concatenate([xp_ref[...], x, xn_ref[...]], axis=0)
    h = _rms(xa, g_ref[...]).astype(BF16)
    row = lax.broadcasted_iota(jnp.int32, (tm + 2 * HALO, 1), 0)
    outside = jnp.logical_or(jnp.logical_and(is_first, row < HALO),
                             jnp.logical_and(is_last, row >= tm + HALO))
    n_sh = tm + 2 * HALO - SUBLANES
    for cb in range(D_MODEL // CONV_COLS):
        cs = slice(cb * CONV_COLS, (cb + 1) * CONV_COLS)
        gs = slice(D_MODEL + cb * CONV_COLS, D_MODEL + (cb + 1) * CONV_COLS)
        val = _dot(h, w1_ref[:, cs]) + b1_ref[:, cs]
        gate = _dot(h, w1_ref[:, gs]) + b1_ref[:, gs]
        glu_ref[cb] = jnp.where(outside, 0.0, val * jax.nn.sigmoid(gate))
        for b in range(SUBLANES):
            sh_ref[b] = glu_ref[cb, pl.ds(b, n_sh), :]
        acc = jnp.zeros((tm, CONV_COLS), F32) + bdw_ref[:, cs]
        for j in range(CONV_WIDTH):
            a8, b = divmod(HALO - CONV_PAD + j, SUBLANES)
            acc = acc + sh_ref[b, pl.ds(a8 * SUBLANES, tm), :] * wdw_ref[j:j + 1, cs]
        dw_ref[:, cs] = acc
    y = _rms(dw_ref[...], gn_ref[...])
    y = (y * jax.nn.sigmoid(y)).astype(BF16)
    o_ref[...] = x + _dot(y, w2_ref[...]) + b2_ref[...]


def _conv(x, groups, g, w_pw1, b_pw1, w_dw, b_dw, gn, w_pw2, b_pw2, *, tm):
    T, D = x.shape
    seq_starts = tuple(off + b * s for off, nb, s in groups for b in range(nb))
    seq_ends = tuple(off + (b + 1) * s for off, nb, s in groups for b in range(nb))
    for off, nb, s in groups:
        assert s % tm == 0 and off % tm == 0
    r = tm // HALO
    last_halo = T // HALO - 1
    kern = functools.partial(_conv_kernel, seq_starts=seq_starts, seq_ends=seq_ends)
    return pl.pallas_call(
        kern,
        grid=(T // tm,),
        in_specs=[pl.BlockSpec((tm, D), lambda i: (i, 0)),
                  pl.BlockSpec((HALO, D), lambda i: (jnp.maximum(i * r - 1, 0), 0)),
                  pl.BlockSpec((HALO, D), lambda i: (jnp.minimum((i + 1) * r, last_halo), 0)),
                  _resident((1, D)), _resident(w_pw1.shape), _resident((1, 2 * D)),
                  _resident((CONV_WIDTH, D)), _resident((1, D)), _resident((1, D)),
                  _resident(w_pw2.shape), _resident((1, D))],
        out_specs=pl.BlockSpec((tm, D), lambda i: (i, 0)),
        out_shape=jax.ShapeDtypeStruct((T, D), F32),
        scratch_shapes=[pltpu.VMEM((D // CONV_COLS, tm + 2 * HALO, CONV_COLS), F32),
                        pltpu.VMEM((SUBLANES, tm + 2 * HALO - SUBLANES, CONV_COLS), F32),
                        pltpu.VMEM((tm, D), F32)],
        compiler_params=_params(1, 48),
        name="conv",
    )(x, x, x, g.reshape(1, D), w_pw1.astype(BF16), b_pw1.reshape(1, 2 * D),
      w_dw.reshape(CONV_WIDTH, D), b_dw.reshape(1, D), gn.reshape(1, D),
      w_pw2.astype(BF16), b_pw2.reshape(1, D))


def kernel(x_prompt, x_sample, norm_g, final_g, ffn1_wg, ffn1_wu, ffn1_wd, ffn2_wg, ffn2_wu, ffn2_wd,
           mla_wq_a, mla_gq, mla_wq_b, mla_wkv_a, mla_gkv, mla_wkv_b, mla_wo,
           fnet_wo, fnet_bo, sgu_w_in, sgu_gv, sgu_ws, sgu_bs, sgu_w_out,
           conv_w_pw1, conv_b_pw1, conv_w_dw, conv_b_dw, conv_gn, conv_w_pw2, conv_b_pw2):
    bp, sp, D = x_prompt.shape
    bs_, ss, _ = x_sample.shape
    assert D == D_MODEL
    n_p, n_s = bp * sp, bs_ * ss
    groups = ((0, bp, sp), (n_p, bs_, ss))
    n_tok = n_p + n_s
    tm = 512
    depth = norm_g.shape[0]
    assert depth == 4

    def ffn_w(wg, wu, wd, i):
        return wg[i].astype(BF16), wu[i].astype(BF16), wd[i].astype(BF16)

    def ffn(x, *a, **kw):
        return _ffn(x, *a, tm=FFN_TM, **kw)

    w0 = ffn_w(ffn1_wg, ffn1_wu, ffn1_wd, 0)
    x = ffn(x_prompt.reshape(n_p, D), norm_g[0, 0], *w0, out_rows=n_tok)
    x = ffn(x_sample.reshape(n_s, D), norm_g[0, 0], *w0, out_rows=n_tok, out_start=n_p, prev=x)
    o = _mla(x, groups, norm_g[0, 1], mla_wq_a[0], mla_gq[0], mla_wq_b[0], mla_wkv_a[0], mla_gkv[0],
             mla_wkv_b[0], **MLA_TILES)
    x = ffn(x, norm_g[0, 2], *ffn_w(ffn2_wg, ffn2_wu, ffn2_wd, 0), pre=(o, mla_wo[0].astype(BF16)))
    x = ffn(x, norm_g[1, 0], *ffn_w(ffn1_wg, ffn1_wu, ffn1_wd, 1))
    x = _fnet(x, groups, norm_g[1, 1], fnet_wo[0], fnet_bo[0])
    x = ffn(x, norm_g[1, 2], *ffn_w(ffn2_wg, ffn2_wu, ffn2_wd, 1))
    x = ffn(x, norm_g[2, 0], *ffn_w(ffn1_wg, ffn1_wu, ffn1_wd, 2))
    x = _sgu(x, norm_g[2, 1], sgu_w_in[0], sgu_gv[0], sgu_ws[0], sgu_bs[0], sgu_w_out[0], tm=tm)
    x = ffn(x, norm_g[2, 2], *ffn_w(ffn2_wg, ffn2_wu, ffn2_wd, 2))
    x = ffn(x, norm_g[3, 0], *ffn_w(ffn1_wg, ffn1_wu, ffn1_wd, 3))
    x = _conv(x, groups, norm_g[3, 1], conv_w_pw1[0], conv_b_pw1[0], conv_w_dw[0], conv_b_dw[0],
              conv_gn[0], conv_w_pw2[0], conv_b_pw2[0], tm=tm)
    w3 = ffn_w(ffn2_wg, ffn2_wu, ffn2_wd, 3)
    y_p = ffn(x, norm_g[3, 2], *w3, row_start=0, n_rows=n_p, final_g=final_g)
    y_s = ffn(x, norm_g[3, 2], *w3, row_start=n_p, n_rows=n_s, final_g=final_g)
    return y_p.reshape(bp, sp, D), y_s.reshape(bs_, ss, D)
```

```python
import functools
import math

import numpy as np
import jax
import jax.numpy as jnp
from jax import lax
from jax.experimental import pallas as pl
from jax.experimental.pallas import tpu as pltpu

F32 = jnp.float32
BF16 = jnp.bfloat16

EPS = 1e-6
D_MODEL = 1024
D_FF = 2816
FF_CHUNKS = (1024, 1024, 768)
FFN_TM = 1024
FFN_VMEM_MIB = 56
HEADS = 16
QK_NOPE = 64
QK_ROPE = 32
V_HEAD = 64
Q_LORA = 384
KV_LORA = 256
ROPE_THETA = 10000.0
ATTN_SCALE = 1.0 / math.sqrt(QK_NOPE + QK_ROPE)
HEAD_PAD = 128
MAX_HI_COL = QK_NOPE + QK_ROPE
MAX_LO_COL = MAX_HI_COL + 1
KEY_UNIT = 256
LOG2E = math.log2(math.e)
NORM_ROWS = 2048
SAFE_BOUND_LOG2 = 30.0 * LOG2E
MLA_TILES = dict(tm=512, tq=1024, unroll=16)
FNET_GROUPS = 4
FNET_GC = D_MODEL // FNET_GROUPS
SGU_CHUNK = 128
SGU_HALF = 2 * D_MODEL
SGU_GROUPS = 4
SGU_GC = SGU_HALF // SGU_GROUPS
CONV_WIDTH = 31
CONV_PAD = (CONV_WIDTH - 1) // 2
HALO = 16
CONV_COLS = 256
SUBLANES = 8
LANES = 128
MIB = 1 << 20


def _params(n_axes, vmem_mib):
    return pltpu.CompilerParams(
        dimension_semantics=("arbitrary",) * n_axes,
        vmem_limit_bytes=vmem_mib * MIB)


def _resident(shape):
    nd = len(shape)
    return pl.BlockSpec(shape, lambda *_: (0,) * nd, pipeline_mode=pl.Buffered(1))


def _rms(x, g):
    return x * lax.rsqrt(jnp.mean(x * x, axis=-1, keepdims=True) + EPS) * g


def _dot(a, b):
    return jnp.dot(a, b, preferred_element_type=F32)


def _ffn_body(x, g_ref, wg_ref, wu_ref, wd_ref):
    h = _rms(x, g_ref[...]).astype(BF16)
    acc = None
    off = 0
    for c in FF_CHUNKS:
        gate = _dot(h, wg_ref[:, off:off + c])
        up = _dot(h, wu_ref[:, off:off + c])
        a = (gate * jax.nn.sigmoid(gate) * up).astype(BF16)
        part = _dot(a, wd_ref[off:off + c, :])
        acc = part if acc is None else acc + part
        off += c
    return x + 0.5 * acc


def _ffn_kernel(*refs, n_first, has_pre, has_final):
    refs = list(refs)
    x = refs.pop(0)[...]
    if n_first is not None:
        x = jnp.where(pl.program_id(0) < n_first, x, refs.pop(0)[...])
    if has_pre:
        o_in_ref, wo_ref = refs.pop(0), refs.pop(0)
        x = x + _dot(o_in_ref[...], wo_ref[...])
    g_ref, wg_ref, wu_ref, wd_ref = (refs.pop(0) for _ in range(4))
    fg_ref = refs.pop(0) if has_final else None
    (o_ref,) = refs
    y = _ffn_body(x, g_ref, wg_ref, wu_ref, wd_ref)
    o_ref[...] = _rms(y, fg_ref[...]) if has_final else y


def _ffn(x, g, wg, wu, wd, *, tm, row_start=0, n_rows=None, final_g=None, pre=None,
         second=None, in_place=False):
    T, D = x.shape
    n_rows = T if n_rows is None else n_rows
    assert n_rows % tm == 0 and row_start % tm == 0
    blk0, n1 = row_start // tm, n_rows // tm
    args = [x]
    if second is None:
        n_first, n_steps = None, n1
        in_specs = [pl.BlockSpec((tm, D), lambda i: (blk0 + i, 0))]
    else:
        assert row_start == 0 and n_rows == T and second.shape[0] % tm == 0 and not in_place
        n_first, n_steps = n1, n1 + second.shape[0] // tm
        args.append(second)
        in_specs = [pl.BlockSpec((tm, D), lambda i: (jnp.minimum(i, n1 - 1), 0)),
                    pl.BlockSpec((tm, D), lambda i: (jnp.maximum(i - n1, 0), 0))]
    if pre is not None:
        o_in, wo = pre
        args += [o_in, wo]
        in_specs += [pl.BlockSpec((tm, o_in.shape[1]), lambda i: (i, 0)), _resident(wo.shape)]
    args += [g.reshape(1, D), wg, wu, wd]
    in_specs += [_resident((1, D)), _resident(wg.shape), _resident(wu.shape), _resident(wd.shape)]
    if final_g is not None:
        args.append(final_g.reshape(1, D))
        in_specs.append(_resident((1, D)))
    oblk0 = blk0 if in_place else 0
    out_rows = T if in_place else n_steps * tm
    body = functools.partial(_ffn_kernel, n_first=n_first, has_pre=pre is not None,
                             has_final=final_g is not None)
    return pl.pallas_call(
        body,
        grid=(n_steps,),
        in_specs=in_specs,
        out_specs=pl.BlockSpec((tm, D), lambda i: (oblk0 + i, 0)),
        out_shape=jax.ShapeDtypeStruct((out_rows, D), F32),
        input_output_aliases={0: 0} if in_place else {},
        compiler_params=_params(1, FFN_VMEM_MIB),
        name="ffn",
    )(*args)


def _mla_proj_kernel(x_ref, g_ref, wqa_ref, gq_ref, wq1_ref, wq2_ref, wkva_ref, gkv_ref,
                     wkr_ref, wk_ref, wv_ref, cq_ref, sq_ref, ck_ref, sk_ref, kone_ref,
                     q_ref, k_ref, v_ref):
    h = _rms(x_ref[...], g_ref[...]).astype(BF16)
    cq = _rms(_dot(h, wqa_ref[...]), gq_ref[...]).astype(BF16)
    ckv = _rms(_dot(h, wkva_ref[...]), gkv_ref[...]).astype(BF16)
    kr2 = _dot(h, wkr_ref[...])
    kr = kr2[:, :HEAD_PAD] * ck_ref[...] + kr2[:, HEAD_PAD:] * sk_ref[...] + kone_ref[...]
    kr = jnp.concatenate([kr, kr], axis=1)
    cq_t = jnp.concatenate([cq_ref[...]] * 2, axis=1)
    sq_t = jnp.concatenate([sq_ref[...]] * 2, axis=1)
    for hp in range(HEADS // 2):
        sl = slice(2 * hp * HEAD_PAD, 2 * (hp + 1) * HEAD_PAD)
        q2 = (_dot(cq, wq1_ref[:, sl]) * cq_t + _dot(cq, wq2_ref[:, sl]) * sq_t).astype(BF16)
        k2 = (_dot(ckv, wk_ref[:, sl]) + kr).astype(BF16)
        for e in range(2):
            q_ref[2 * hp + e] = q2[:, e * HEAD_PAD:(e + 1) * HEAD_PAD]
            k_ref[2 * hp + e] = k2[:, e * HEAD_PAD:(e + 1) * HEAD_PAD]
    v = _dot(ckv, wv_ref[...])
    for hp in range(HEADS // 2):
        v_ref[hp] = v[:, hp * LANES:(hp + 1) * LANES].astype(BF16)


def _row_sums(x, terms):
    ones = jnp.ones((LANES, LANES), BF16)
    total = None
    for _ in range(terms):
        piece = x.astype(BF16)
        part = _dot(piece, ones)
        total = part if total is None else total + part
        x = x - piece.astype(F32)
    return total


def _qk(q, k_ref, head, start):
    ks = k_ref[head, pl.ds(start, KEY_UNIT), :]
    return lax.dot_general(q, ks, (((1,), (1,)), ((), ())), preferred_element_type=F32)


def _attn_kernel(q_ref, k_ref, v_ref, o_ref, knorm_ref, shift_ref, *, seq, unroll):
    tq = q_ref.shape[1]
    n_iter = seq // (KEY_UNIT * unroll)
    lane = lax.broadcasted_iota(jnp.int32, (tq, LANES), 1)

    @pl.when(pl.program_id(2) == 0)
    def _key_norms():
        for j in range(2):
            def body(c, mx, j=j):
                start = pl.multiple_of(c * NORM_ROWS, NORM_ROWS)
                kk = k_ref[j, pl.ds(start, NORM_ROWS), :].astype(F32)
                return jnp.maximum(mx, _row_sums(kk * kk, 2))

            mx = lax.fori_loop(0, seq // NORM_ROWS, body, jnp.zeros((NORM_ROWS, LANES), F32))
            knorm_ref[j] = jnp.broadcast_to(jnp.max(mx, axis=0, keepdims=True), (SUBLANES, LANES))

    outs = []
    for j in range(2):
        q = q_ref[j]
        qf = q.astype(F32)
        bound = jnp.sqrt(_row_sums(qf * qf, 2) * knorm_ref[j][0:1, :])
        shift_ref[...] = bound

        @pl.when(jnp.max(bound) > SAFE_BOUND_LOG2)
        def _exact_max(j=j, q=q):
            def max_body(c, mx):
                for u in range(unroll):
                    start = pl.multiple_of((c * unroll + u) * KEY_UNIT, KEY_UNIT)
                    s = _qk(q, k_ref, j, start)
                    mx = jnp.maximum(mx, jnp.maximum(s[:, :LANES], s[:, LANES:]))
                return mx

            mx = lax.fori_loop(0, n_iter, max_body, jnp.full((tq, LANES), -jnp.inf, F32))
            shift_ref[...] = jnp.broadcast_to(jnp.max(mx, axis=-1, keepdims=True), (tq, LANES))

        m = shift_ref[...]
        m_hi = m.astype(BF16).astype(F32)
        m_lo = m - m_hi
        qa = jnp.where(lane == MAX_HI_COL, -m_hi, jnp.where(lane == MAX_LO_COL, -m_lo, q.astype(F32)))
        qa = qa.astype(BF16)

        def acc_body(c, carry, j=j, qa=qa):
            l, acc = carry
            for u in range(unroll):
                start = pl.multiple_of((c * unroll + u) * KEY_UNIT, KEY_UNIT)
                p = jnp.exp2(_qk(qa, k_ref, j, start))
                l = l + (p[:, :LANES] + p[:, LANES:])
                acc = acc + _dot(p.astype(BF16), v_ref[0, pl.ds(start, KEY_UNIT), :])
            return l, acc

        zeros = jnp.zeros((tq, LANES), F32)
        l, acc = lax.fori_loop(0, n_iter, acc_body, (zeros, zeros))
        outs.append(acc / _row_sums(l, 3))
    o_ref[...] = jnp.where(lane < V_HEAD, outs[0], outs[1]).astype(o_ref.dtype)


def _rope_tables(seq_len):
    pos = jnp.arange(seq_len, dtype=F32)
    inv_freq = 1.0 / (ROPE_THETA ** (jnp.arange(0, QK_ROPE, 2, dtype=F32) / QK_ROPE))
    ang = pos[:, None] * inv_freq[None, :]
    cos, sin = jnp.cos(ang), jnp.sin(ang)
    ones = jnp.ones((seq_len, QK_NOPE), F32)
    zeros_n = jnp.zeros((seq_len, QK_NOPE), F32)
    zeros_p = jnp.zeros((seq_len, HEAD_PAD - QK_NOPE - QK_ROPE), F32)
    c_tab = jnp.concatenate([ones, cos, cos, zeros_p], axis=1)
    s_tab = jnp.concatenate([zeros_n, -sin, sin, zeros_p], axis=1)
    return c_tab, s_tab


def _swap_halves(w):
    half = w.shape[-1] // 2
    return jnp.concatenate([w[..., half:], w[..., :half]], axis=-1)


def _mla(x, groups, g, wq_a, gq, wq_b, wkv_a, gkv, wkv_b, *, tm, tq, unroll):
    T, D = x.shape
    pad = HEAD_PAD - QK_NOPE - QK_ROPE
    wq3 = wq_b.reshape(Q_LORA, HEADS, QK_NOPE + QK_ROPE)
    qn, qr = wq3[..., :QK_NOPE], wq3[..., QK_NOPE:]
    zp = jnp.zeros((Q_LORA, HEADS, pad), F32)
    wq1 = jnp.concatenate([qn, qr, zp], -1).reshape(Q_LORA, HEADS * HEAD_PAD).astype(BF16)
    wq2 = jnp.concatenate([jnp.zeros_like(qn), _swap_halves(qr), zp], -1).reshape(Q_LORA, HEADS * HEAD_PAD).astype(BF16)
    wkva_c = wkv_a[:, :KV_LORA].astype(BF16)
    wr = wkv_a[:, KV_LORA:]
    zn = jnp.zeros((D, QK_NOPE), F32)
    zpd = jnp.zeros((D, pad), F32)
    wkr = jnp.concatenate([zn, wr, zpd, zn, _swap_halves(wr), zpd], axis=1).astype(BF16)
    wkv3 = wkv_b.reshape(KV_LORA, HEADS, QK_NOPE + V_HEAD)
    wk = jnp.concatenate([wkv3[..., :QK_NOPE], jnp.zeros((KV_LORA, HEADS, HEAD_PAD - QK_NOPE), F32)], -1)
    wk = wk.reshape(KV_LORA, HEADS * HEAD_PAD).astype(BF16)
    wv = wkv3[..., QK_NOPE:].reshape(KV_LORA, HEADS * V_HEAD).astype(BF16)

    s_max = max(s for _, _, s in groups)
    c_tab, s_tab = _rope_tables(s_max)
    cq_tab, sq_tab = c_tab * (ATTN_SCALE * LOG2E), s_tab * (ATTN_SCALE * LOG2E)
    kone = jnp.zeros((1, HEAD_PAD), F32).at[0, MAX_HI_COL:MAX_LO_COL + 1].set(1.0)

    bounds = [(off // tm, (off + b * s) // tm, s // tm) for off, b, s in groups]

    def pos_block(i):
        blk = i
        for lo, _, per in bounds:
            blk = jnp.where(i >= lo, (i - lo) % per, blk)
        return (blk, 0)

    for off, b, s in groups:
        assert s % tm == 0 and off % tm == 0
    tab_spec = pl.BlockSpec((tm, HEAD_PAD), pos_block)
    q, k, v = pl.pallas_call(
        _mla_proj_kernel,
        grid=(T // tm,),
        in_specs=[pl.BlockSpec((tm, D), lambda i: (i, 0)), _resident((1, D)),
                  _resident(wq_a.shape), _resident((1, Q_LORA)), _resident(wq1.shape), _resident(wq2.shape),
                  _resident(wkva_c.shape), _resident((1, KV_LORA)), _resident(wkr.shape),
                  _resident(wk.shape), _resident(wv.shape),
                  tab_spec, tab_spec, tab_spec, tab_spec, _resident((1, HEAD_PAD))],
        out_specs=[pl.BlockSpec((HEADS, tm, HEAD_PAD), lambda i: (0, i, 0)),
                   pl.BlockSpec((HEADS, tm, HEAD_PAD), lambda i: (0, i, 0)),
                   pl.BlockSpec((HEADS // 2, tm, LANES), lambda i: (0, i, 0))],
        out_shape=[jax.ShapeDtypeStruct((HEADS, T, HEAD_PAD), BF16),
                   jax.ShapeDtypeStruct((HEADS, T, HEAD_PAD), BF16),
                   jax.ShapeDtypeStruct((HEADS // 2, T, LANES), BF16)],
        compiler_params=_params(1, 48),
        name="mla_proj",
    )(x, g.reshape(1, D), wq_a.astype(BF16), gq.reshape(1, Q_LORA), wq1, wq2,
      wkva_c, gkv.reshape(1, KV_LORA), wkr, wk, wv, cq_tab, sq_tab, c_tab, s_tab, kone)

    outs = []
    for off, nb, s in groups:
        n_unroll = min(unroll, s // KEY_UNIT)
        assert off % s == 0 and s % tq == 0 and s % (KEY_UNIT * n_unroll) == 0
        seq0 = off // s
        nq = s // tq
        q0 = off // tq
        in_specs = [pl.BlockSpec((2, tq, HEAD_PAD), lambda b, hp, qi: (hp, q0 + b * nq + qi, 0)),
                    pl.BlockSpec((2, s, HEAD_PAD), lambda b, hp, qi: (hp, seq0 + b, 0)),
                    pl.BlockSpec((1, s, LANES), lambda b, hp, qi: (hp, seq0 + b, 0))]
        outs.append(pl.pallas_call(
            functools.partial(_attn_kernel, seq=s, unroll=n_unroll),
            grid=(nb, HEADS // 2, nq),
            in_specs=in_specs,
            out_specs=pl.BlockSpec((tq, LANES), lambda b, hp, qi: (b * nq + qi, hp)),
            out_shape=jax.ShapeDtypeStruct((nb * s, HEADS * V_HEAD), BF16),
            scratch_shapes=[pltpu.VMEM((2, SUBLANES, LANES), F32), pltpu.VMEM((tq, LANES), F32)],
            compiler_params=_params(3, 48),
            name="mla_attn",
        )(q, k, v))
    return outs


def _proj_res_kernel(x_ref, a_ref, w_ref, o_ref):
    o_ref[...] = x_ref[...] + _dot(a_ref[...], w_ref[...])


def _proj_res(x, a, w, *, tm):
    T, D = x.shape
    return pl.pallas_call(
        _proj_res_kernel,
        grid=(T // tm,),
        in_specs=[pl.BlockSpec((tm, D), lambda i: (i, 0)),
                  pl.BlockSpec((tm, a.shape[1]), lambda i: (i, 0)), _resident(w.shape)],
        out_specs=pl.BlockSpec((tm, D), lambda i: (i, 0)),
        out_shape=jax.ShapeDtypeStruct((T, D), F32),
        compiler_params=_params(1, 40),
        name="proj_res",
    )(x, a, w)


def _dft_cos_sin(n):
    idx = np.arange(n)
    ang = 2.0 * np.pi * ((idx[:, None] * idx[None, :]) % n) / n
    scale = 1.0 / math.sqrt(n)
    return np.cos(ang) * scale, np.sin(ang) * scale


def _fnet_factor(s):
    na = 1 << (int(math.log2(s)) // 2)
    nb = s // na
    assert na * nb == s and nb % (2 * SUBLANES) == 0 and na % SUBLANES == 0
    return na, nb


def _fnet_stage1_kernel(x_ref, g_ref, cc_ref, nsc_ref, k1c_ref, k1s_ref, twc_ref, tws_ref,
                        ar_ref, ai_ref):
    na = x_ref.shape[0]
    rows = na * SUBLANES
    halves_r, halves_i = [], []
    for half in range(2):
        sub = slice(half * SUBLANES, (half + 1) * SUBLANES)
        xb = x_ref[:, sub, :].reshape(rows, D_MODEL)
        h = _rms(xb, g_ref[...]).astype(BF16)
        twc = twc_ref[:, sub, :].reshape(rows, LANES)
        tws = tws_ref[:, sub, :].reshape(rows, LANES)
        twc = jnp.concatenate([twc] * (FNET_GC // LANES), axis=1)
        tws = jnp.concatenate([tws] * (FNET_GC // LANES), axis=1)
        cols_r, cols_i = [], []
        for grp in range(FNET_GROUPS):
            hg = h[:, grp * FNET_GC:(grp + 1) * FNET_GC]
            zr = _dot(hg, cc_ref[...]).astype(BF16)
            zi = _dot(hg, nsc_ref[...]).astype(BF16)
            k1c = k1c_ref[...]
            k1s = k1s_ref[...]
            a_r = _dot(k1c, zr) + _dot(k1s, zi)
            a_i = _dot(k1c, zi) - _dot(k1s, zr)
            cols_r.append(a_r * twc + a_i * tws)
            cols_i.append(a_i * twc - a_r * tws)
        halves_r.append(jnp.concatenate(cols_r, axis=1).reshape(na, SUBLANES, D_MODEL))
        halves_i.append(jnp.concatenate(cols_i, axis=1).reshape(na, SUBLANES, D_MODEL))
    ar_ref[...] = jnp.concatenate(halves_r, axis=1).astype(BF16)
    ai_ref[...] = jnp.concatenate(halves_i, axis=1).astype(BF16)


def _fnet_stage2_kernel(x_ref, ar_ref, ai_ref, gc_ref, gs_ref, wo_ref, bo_ref, o_ref):
    nb = x_ref.shape[0]
    rows = nb * SUBLANES
    y = _dot(gc_ref[...], ar_ref[...]) + _dot(gs_ref[...], ai_ref[...])
    m = _dot(y.astype(BF16), wo_ref[...]) + bo_ref[...]
    o_ref[...] = (x_ref[...].reshape(rows, D_MODEL) + m).reshape(nb, SUBLANES, D_MODEL)


def _fnet(x, groups, g, wo, bo):
    T, D = x.shape
    cc, sc = _dft_cos_sin(FNET_GC)
    cc_j = jnp.asarray(cc, F32).astype(BF16)
    nsc_j = jnp.asarray(-sc, F32).astype(BF16)
    wo_b = wo.astype(BF16)
    for off, nbatch, s in groups:
        assert off % s == 0
        na, nb = _fnet_factor(s)
        seq0 = off // s
        c1, s1 = _dft_cos_sin(na)
        eye = np.eye(SUBLANES)
        k1c = jnp.asarray(np.kron(c1, eye), F32).astype(BF16)
        k1s = jnp.asarray(np.kron(s1, eye), F32).astype(BF16)
        c2, s2 = _dft_cos_sin(nb)
        gc = np.zeros((nb, SUBLANES, SUBLANES, nb))
        gs = np.zeros((nb, SUBLANES, SUBLANES, nb))
        for j in range(SUBLANES):
            gc[:, j, j, :] = c2
            gs[:, j, j, :] = s2
        n2 = nb * SUBLANES
        gc_j = jnp.asarray(gc.reshape(n2, n2), F32).astype(BF16)
        gs_j = jnp.asarray(gs.reshape(n2, n2), F32).astype(BF16)
        k1i = jnp.arange(na, dtype=jnp.int32)[:, None]
        bi = jnp.arange(nb, dtype=jnp.int32)[None, :]
        ang = ((k1i * bi) % s).astype(F32) * (2.0 * math.pi / s)
        twc = jnp.broadcast_to(jnp.cos(ang)[:, :, None], (na, nb, LANES))
        tws = jnp.broadcast_to(jnp.sin(ang)[:, :, None], (na, nb, LANES))

        tb = 2 * SUBLANES
        n1 = na * SUBLANES
        x3 = x.reshape(T // nb, nb, D)
        blk3 = pl.BlockSpec((na, tb, D), lambda b, j: (seq0 + b, j, 0))
        tw_spec = pl.BlockSpec((na, tb, LANES), lambda b, j: (0, j, 0))
        a_r, a_i = pl.pallas_call(
            _fnet_stage1_kernel,
            grid=(nbatch, nb // tb),
            in_specs=[blk3, _resident((1, D)), _resident(cc_j.shape), _resident(nsc_j.shape),
                      _resident((n1, n1)), _resident((n1, n1)), tw_spec, tw_spec],
            out_specs=[pl.BlockSpec((na, tb, D), lambda b, j: (b, j, 0))] * 2,
            out_shape=[jax.ShapeDtypeStruct((nbatch * na, nb, D), BF16)] * 2,
            compiler_params=_params(2, 56),
            name="fnet_stage1",
        )(x3, g.reshape(1, D), cc_j, nsc_j, k1c, k1s, twc, tws)

        a_r = a_r.reshape(nbatch * s, D)
        a_i = a_i.reshape(nbatch * s, D)
        xo = x.reshape(T // na, na, D)
        xblk = pl.BlockSpec((nb, SUBLANES, D), lambda b, j: (seq0 + b, j, 0))
        ablk = pl.BlockSpec((n2, D), lambda b, j: (b * (na // SUBLANES) + j, 0))
        xo = pl.pallas_call(
            _fnet_stage2_kernel,
            grid=(nbatch, na // SUBLANES),
            in_specs=[xblk, ablk, ablk, _resident((n2, n2)), _resident((n2, n2)),
                      _resident(wo_b.shape), _resident((1, D))],
            out_specs=xblk,
            out_shape=jax.ShapeDtypeStruct(xo.shape, F32),
            input_output_aliases={0: 0},
            compiler_params=_params(2, 56),
            name="fnet_stage2",
        )(xo, a_r, a_i, gc_j, gs_j, wo_b, bo.reshape(1, D))
        x = xo.reshape(T, D)
    return x


def _gelu_tanh(x):
    return 0.5 * x * (1.0 + jnp.tanh(math.sqrt(2.0 / math.pi) * (x + 0.044715 * (x * x * x))))


def _sgu_kernel(x_ref, g_ref, wu_ref, wv_ref, gv_ref, ws_ref, bs_ref, wout_ref, o_ref):
    x = x_ref[...]
    tm = x.shape[0]
    h = _rms(x, g_ref[...]).astype(BF16)
    v = _gelu_tanh(_dot(h, wv_ref[...]))
    v = _rms(v, gv_ref[...]).astype(BF16)
    u = _gelu_tanh(_dot(h, wu_ref[...]))
    rows = []
    for c in range(tm // SGU_CHUNK):
        r = slice(c * SGU_CHUNK, (c + 1) * SGU_CHUNK)
        cols = []
        for grp in range(SGU_GROUPS):
            cs = slice(grp * SGU_GC, (grp + 1) * SGU_GC)
            sv = _dot(ws_ref[grp], v[r, cs]) + bs_ref[grp]
            cols.append((u[r, cs] * sv).astype(BF16))
        rows.append(jnp.concatenate(cols, axis=1))
    gated = jnp.concatenate(rows, axis=0)
    o_ref[...] = x + _dot(gated, wout_ref[...])


def _sgu(x, g, w_in, gv, ws, bs, w_out, *, tm):
    T, D = x.shape
    assert tm % SGU_CHUNK == 0
    wu = w_in[:, :SGU_HALF].astype(BF16)
    wv = w_in[:, SGU_HALF:].astype(BF16)
    bs_b = jnp.broadcast_to(bs[:, :, None], (SGU_GROUPS, SGU_CHUNK, SGU_GC)).astype(F32)
    return pl.pallas_call(
        _sgu_kernel,
        grid=(T // tm,),
        in_specs=[pl.BlockSpec((tm, D), lambda i: (i, 0)), _resident((1, D)),
                  _resident(wu.shape), _resident(wv.shape), _resident((1, SGU_HALF)),
                  _resident(ws.shape), _resident(bs_b.shape), _resident(w_out.shape)],
        out_specs=pl.BlockSpec((tm, D), lambda i: (i, 0)),
        out_shape=jax.ShapeDtypeStruct((T, D), F32),
        compiler_params=_params(1, 52),
        name="sgu",
    )(x, g.reshape(1, D), wu, wv, gv.reshape(1, SGU_HALF), ws.astype(BF16), bs_b, w_out.astype(BF16))


def _conv_kernel(x_ref, xp_ref, xn_ref, g_ref, w1_ref, b1_ref, wdw_ref, bdw_ref, gn_ref, w2_ref, b2_ref,
                 o_ref, glu_ref, sh_ref, dw_ref, *, seq_starts, seq_ends):
    tm = x_ref.shape[0]
    i = pl.program_id(0)
    start = i * tm
    is_first = functools.reduce(jnp.logical_or, [start == s for s in seq_starts])
    is_last = functools.reduce(jnp.logical_or, [start + tm == e for e in seq_ends])
    x = x_ref[...]
    xa = jnp.concatenate([xp_ref[...], x, xn_ref[...]], axis=0)
    h = _rms(xa, g_ref[...]).astype(BF16)
    row = lax.broadcasted_iota(jnp.int32, (tm + 2 * HALO, 1), 0)
    outside = jnp.logical_or(jnp.logical_and(is_first, row < HALO),
                             jnp.logical_and(is_last, row >= tm + HALO))
    n_sh = tm + 2 * HALO - SUBLANES
    for cb in range(D_MODEL // CONV_COLS):
        cs = slice(cb * CONV_COLS, (cb + 1) * CONV_COLS)
        gs = slice(D_MODEL + cb * CONV_COLS, D_MODEL + (cb + 1) * CONV_COLS)
        val = _dot(h, w1_ref[:, cs]) + b1_ref[:, cs]
        gate = _dot(h, w1_ref[:, gs]) + b1_ref[:, gs]
        glu_ref[cb] = jnp.where(outside, 0.0, val * jax.nn.sigmoid(gate))
        for b in range(SUBLANES):
            sh_ref[b] = glu_ref[cb, pl.ds(b, n_sh), :]
        acc = jnp.zeros((tm, CONV_COLS), F32) + bdw_ref[:, cs]
        for j in range(CONV_WIDTH):
            a8, b = divmod(HALO - CONV_PAD + j, SUBLANES)
            acc = acc + sh_ref[b, pl.ds(a8 * SUBLANES, tm), :] * wdw_ref[j:j + 1, cs]
        dw_ref[:, cs] = acc
    y = _rms(dw_ref[...], gn_ref[...])
    y = (y * jax.nn.sigmoid(y)).astype(BF16)
    o_ref[...] = x + _dot(y, w2_ref[...]) + b2_ref[...]


def _conv(x, groups, g, w_pw1, b_pw1, w_dw, b_dw, gn, w_pw2, b_pw2, *, tm):
    T, D = x.shape
    seq_starts = tuple(off + b * s for off, nb, s in groups for b in range(nb))
    seq_ends = tuple(off + (b + 1) * s for off, nb, s in groups for b in range(nb))
    for off, nb, s in groups:
        assert s % tm == 0 and off % tm == 0
    r = tm // HALO
    last_halo = T // HALO - 1
    kern = functools.partial(_conv_kernel, seq_starts=seq_starts, seq_ends=seq_ends)
    return pl.pallas_call(
        kern,
        grid=(T // tm,),
        in_specs=[pl.BlockSpec((tm, D), lambda i: (i, 0)),
                  pl.BlockSpec((HALO, D), lambda i: (jnp.maximum(i * r - 1, 0), 0)),
                  pl.BlockSpec((HALO, D), lambda i: (jnp.minimum((i + 1) * r, last_halo), 0)),
                  _resident((1, D)), _resident(w_pw1.shape), _resident((1, 2 * D)),
                  _resident((CONV_WIDTH, D)), _resident((1, D)), _resident((1, D)),
                  _resident(w_pw2.shape), _resident((1, D))],
        out_specs=pl.BlockSpec((tm, D), lambda i: (i, 0)),
        out_shape=jax.ShapeDtypeStruct((T, D), F32),
        scratch_shapes=[pltpu.VMEM((D // CONV_COLS, tm + 2 * HALO, CONV_COLS), F32),
                        pltpu.VMEM((SUBLANES, tm + 2 * HALO - SUBLANES, CONV_COLS), F32),
                        pltpu.VMEM((tm, D), F32)],
        compiler_params=_params(1, 48),
        name="conv",
    )(x, x, x, g.reshape(1, D), w_pw1.astype(BF16), b_pw1.reshape(1, 2 * D),
      w_dw.reshape(CONV_WIDTH, D), b_dw.reshape(1, D), gn.reshape(1, D),
      w_pw2.astype(BF16), b_pw2.reshape(1, D))


def kernel(x_prompt, x_sample, norm_g, final_g, ffn1_wg, ffn1_wu, ffn1_wd, ffn2_wg, ffn2_wu, ffn2_wd,
           mla_wq_a, mla_gq, mla_wq_b, mla_wkv_a, mla_gkv, mla_wkv_b, mla_wo,
           fnet_wo, fnet_bo, sgu_w_in, sgu_gv, sgu_ws, sgu_bs, sgu_w_out,
           conv_w_pw1, conv_b_pw1, conv_w_dw, conv_b_dw, conv_gn, conv_w_pw2, conv_b_pw2):
    bp, sp, D = x_prompt.shape
    bs_, ss, _ = x_sample.shape
    assert D == D_MODEL
    n_p, n_s = bp * sp, bs_ * ss
    groups = ((0, bp, sp), (n_p, bs_, ss))
    n_tok = n_p + n_s
    tm = 512
    depth = norm_g.shape[0]
    assert depth == 4

    def ffn_w(wg, wu, wd, i):
        return wg[i].astype(BF16), wu[i].astype(BF16), wd[i].astype(BF16)

    def ffn(x, *a, **kw):
        return _ffn(x, *a, tm=FFN_TM, **kw)

    x = _ffn(x_prompt.reshape(n_p, D), norm_g[0, 0], *ffn_w(ffn1_wg, ffn1_wu, ffn1_wd, 0),
             tm=tm, second=x_sample.reshape(n_s, D))
    attn = _mla(x, groups, norm_g[0, 1], mla_wq_a[0], mla_gq[0], mla_wq_b[0], mla_wkv_a[0], mla_gkv[0],
                mla_wkv_b[0], **MLA_TILES)
    w2, wo = ffn_w(ffn2_wg, ffn2_wu, ffn2_wd, 0), mla_wo[0].astype(BF16)
    for (off, nb, s), o in zip(groups, attn):
        x = ffn(x, norm_g[0, 2], *w2, row_start=off, n_rows=nb * s, pre=(o, wo), in_place=True)
    x = ffn(x, norm_g[1, 0], *ffn_w(ffn1_wg, ffn1_wu, ffn1_wd, 1))
    x = _fnet(x, groups, norm_g[1, 1], fnet_wo[0], fnet_bo[0])
    x = ffn(x, norm_g[1, 2], *ffn_w(ffn2_wg, ffn2_wu, ffn2_wd, 1))
    x = ffn(x, norm_g[2, 0], *ffn_w(ffn1_wg, ffn1_wu, ffn1_wd, 2))
    x = _sgu(x, norm_g[2, 1], sgu_w_in[0], sgu_gv[0], sgu_ws[0], sgu_bs[0], sgu_w_out[0], tm=tm)
    x = ffn(x, norm_g[2, 2], *ffn_w(ffn2_wg, ffn2_wu, ffn2_wd, 2))
    x = ffn(x, norm_g[3, 0], *ffn_w(ffn1_wg, ffn1_wu, ffn1_wd, 3))
    x = _conv(x, groups, norm_g[3, 1], conv_w_pw1[0], conv_b_pw1[0], conv_w_dw[0], conv_b_dw[0],
              conv_gn[0], conv_w_pw2[0], conv_b_pw2[0], tm=tm)
    w3 = ffn_w(ffn2_wg, ffn2_wu, ffn2_wd, 3)
    y_p = ffn(x, norm_g[3, 2], *w3, row_start=0, n_rows=n_p, final_g=final_g)
    y_s = ffn(x, norm_g[3, 2], *w3, row_start=n_p, n_rows=n_s, final_g=final_g)
    return y_p.reshape(bp, sp, D), y_s.reshape(bs_, ss, D)
```

```python
import functools
import math

import numpy as np
import jax
import jax.numpy as jnp
from jax import lax
from jax.experimental import pallas as pl
from jax.experimental.pallas import tpu as pltpu

F32 = jnp.float32
BF16 = jnp.bfloat16

EPS = 1e-6
D_MODEL = 1024
D_FF = 2816
FF_CHUNKS = (1024, 1024, 768)
FFN_TM = 1024
FFN_VMEM_MIB = 56
HEADS = 16
QK_NOPE = 64
QK_ROPE = 32
V_HEAD = 64
Q_LORA = 384
KV_LORA = 256
ROPE_THETA = 10000.0
ATTN_SCALE = 1.0 / math.sqrt(QK_NOPE + QK_ROPE)
HEAD_PAD = 128
MAX_HI_COL = QK_NOPE + QK_ROPE
MAX_LO_COL = MAX_HI_COL + 1
KEY_UNIT = 256
LOG2E = math.log2(math.e)
NORM_ROWS = 2048
SAFE_BOUND_LOG2 = 30.0 * LOG2E
MLA_TILES = dict(tm=512, tq=2048, unroll=32)
FNET_GROUPS = 4
FNET_GC = D_MODEL // FNET_GROUPS
SGU_CHUNK = 128
SGU_HALF = 2 * D_MODEL
SGU_GROUPS = 4
SGU_GC = SGU_HALF // SGU_GROUPS
CONV_WIDTH = 31
CONV_PAD = (CONV_WIDTH - 1) // 2
HALO = 16
CONV_COLS = 256
SUBLANES = 8
LANES = 128
MIB = 1 << 20


def _params(n_axes, vmem_mib):
    return pltpu.CompilerParams(
        dimension_semantics=("arbitrary",) * n_axes,
        vmem_limit_bytes=vmem_mib * MIB)


def _resident(shape):
    nd = len(shape)
    return pl.BlockSpec(shape, lambda *_: (0,) * nd, pipeline_mode=pl.Buffered(1))


def _rms(x, g):
    return x * lax.rsqrt(jnp.mean(x * x, axis=-1, keepdims=True) + EPS) * g


def _dot(a, b):
    return jnp.dot(a, b, preferred_element_type=F32)


def _ffn_body(x, g_ref, wg_ref, wu_ref, wd_ref):
    h = _rms(x, g_ref[...]).astype(BF16)
    acc = None
    off = 0
    for c in FF_CHUNKS:
        gate = _dot(h, wg_ref[:, off:off + c])
        up = _dot(h, wu_ref[:, off:off + c])
        a = (gate * jax.nn.sigmoid(gate) * up).astype(BF16)
        part = _dot(a, wd_ref[off:off + c, :])
        acc = part if acc is None else acc + part
        off += c
    return x + 0.5 * acc


def _ffn_kernel(*refs, n_first, has_pre, has_final):
    refs = list(refs)
    x = refs.pop(0)[...]
    if n_first is not None:
        x = jnp.where(pl.program_id(0) < n_first, x, refs.pop(0)[...])
    if has_pre:
        o_in_ref, wo_ref = refs.pop(0), refs.pop(0)
        x = x + _dot(o_in_ref[...], wo_ref[...])
    g_ref, wg_ref, wu_ref, wd_ref = (refs.pop(0) for _ in range(4))
    fg_ref = refs.pop(0) if has_final else None
    (o_ref,) = refs
    y = _ffn_body(x, g_ref, wg_ref, wu_ref, wd_ref)
    o_ref[...] = _rms(y, fg_ref[...]) if has_final else y


def _ffn(x, g, wg, wu, wd, *, tm, row_start=0, n_rows=None, final_g=None, pre=None,
         second=None, in_place=False):
    T, D = x.shape
    n_rows = T if n_rows is None else n_rows
    assert n_rows % tm == 0 and row_start % tm == 0
    blk0, n1 = row_start // tm, n_rows // tm
    args = [x]
    if second is None:
        n_first, n_steps = None, n1
        in_specs = [pl.BlockSpec((tm, D), lambda i: (blk0 + i, 0))]
    else:
        assert row_start == 0 and n_rows == T and second.shape[0] % tm == 0 and not in_place
        n_first, n_steps = n1, n1 + second.shape[0] // tm
        args.append(second)
        in_specs = [pl.BlockSpec((tm, D), lambda i: (jnp.minimum(i, n1 - 1), 0)),
                    pl.BlockSpec((tm, D), lambda i: (jnp.maximum(i - n1, 0), 0))]
    if pre is not None:
        o_in, wo = pre
        args += [o_in, wo]
        in_specs += [pl.BlockSpec((tm, o_in.shape[1]), lambda i: (i, 0)), _resident(wo.shape)]
    args += [g.reshape(1, D), wg, wu, wd]
    in_specs += [_resident((1, D)), _resident(wg.shape), _resident(wu.shape), _resident(wd.shape)]
    if final_g is not None:
        args.append(final_g.reshape(1, D))
        in_specs.append(_resident((1, D)))
    oblk0 = blk0 if in_place else 0
    out_rows = T if in_place else n_steps * tm
    body = functools.partial(_ffn_kernel, n_first=n_first, has_pre=pre is not None,
                             has_final=final_g is not None)
    return pl.pallas_call(
        body,
        grid=(n_steps,),
        in_specs=in_specs,
        out_specs=pl.BlockSpec((tm, D), lambda i: (oblk0 + i, 0)),
        out_shape=jax.ShapeDtypeStruct((out_rows, D), F32),
        input_output_aliases={0: 0} if in_place else {},
        compiler_params=_params(1, FFN_VMEM_MIB),
        name="ffn",
    )(*args)


def _mla_proj_kernel(x_ref, g_ref, wqa_ref, gq_ref, wq1_ref, wq2_ref, wkva_ref, gkv_ref,
                     wkr_ref, wk_ref, wv_ref, cq_ref, sq_ref, ck_ref, sk_ref, kone_ref,
                     q_ref, k_ref, v_ref):
    h = _rms(x_ref[...], g_ref[...]).astype(BF16)
    cq = _rms(_dot(h, wqa_ref[...]), gq_ref[...]).astype(BF16)
    ckv = _rms(_dot(h, wkva_ref[...]), gkv_ref[...]).astype(BF16)
    kr2 = _dot(h, wkr_ref[...])
    kr = kr2[:, :HEAD_PAD] * ck_ref[...] + kr2[:, HEAD_PAD:] * sk_ref[...] + kone_ref[...]
    kr = jnp.concatenate([kr, kr], axis=1)
    cq_t = jnp.concatenate([cq_ref[...]] * 2, axis=1)
    sq_t = jnp.concatenate([sq_ref[...]] * 2, axis=1)
    for hp in range(HEADS // 2):
        sl = slice(2 * hp * HEAD_PAD, 2 * (hp + 1) * HEAD_PAD)
        q2 = (_dot(cq, wq1_ref[:, sl]) * cq_t + _dot(cq, wq2_ref[:, sl]) * sq_t).astype(BF16)
        k2 = (_dot(ckv, wk_ref[:, sl]) + kr).astype(BF16)
        for e in range(2):
            q_ref[2 * hp + e] = q2[:, e * HEAD_PAD:(e + 1) * HEAD_PAD]
            k_ref[2 * hp + e] = k2[:, e * HEAD_PAD:(e + 1) * HEAD_PAD]
    v = _dot(ckv, wv_ref[...])
    for hp in range(HEADS // 2):
        v_ref[hp] = v[:, hp * LANES:(hp + 1) * LANES].astype(BF16)


def _row_sums(x, terms):
    ones = jnp.ones((LANES, LANES), BF16)
    total = None
    for _ in range(terms):
        piece = x.astype(BF16)
        part = _dot(piece, ones)
        total = part if total is None else total + part
        x = x - piece.astype(F32)
    return total


def _qk(q, k_ref, head, start):
    ks = k_ref[head, pl.ds(start, KEY_UNIT), :]
    return lax.dot_general(q, ks, (((1,), (1,)), ((), ())), preferred_element_type=F32)


def _attn_kernel(q_ref, k_ref, v_ref, o_ref, knorm_ref, shift_ref, *, seq, unroll):
    tq = q_ref.shape[1]
    n_iter = seq // (KEY_UNIT * unroll)
    lane = lax.broadcasted_iota(jnp.int32, (tq, LANES), 1)

    @pl.when(pl.program_id(2) == 0)
    def _key_norms():
        for j in range(2):
            def body(c, mx, j=j):
                start = pl.multiple_of(c * NORM_ROWS, NORM_ROWS)
                kk = k_ref[j, pl.ds(start, NORM_ROWS), :].astype(F32)
                return jnp.maximum(mx, _row_sums(kk * kk, 2))

            mx = lax.fori_loop(0, seq // NORM_ROWS, body, jnp.zeros((NORM_ROWS, LANES), F32))
            knorm_ref[j] = jnp.broadcast_to(jnp.max(mx, axis=0, keepdims=True), (SUBLANES, LANES))

    outs = []
    for j in range(2):
        q = q_ref[j]
        qf = q.astype(F32)
        bound = jnp.sqrt(_row_sums(qf * qf, 2) * knorm_ref[j][0:1, :])
        shift_ref[...] = bound

        @pl.when(jnp.max(bound) > SAFE_BOUND_LOG2)
        def _exact_max(j=j, q=q):
            def max_body(c, mx):
                for u in range(unroll):
                    start = pl.multiple_of((c * unroll + u) * KEY_UNIT, KEY_UNIT)
                    s = _qk(q, k_ref, j, start)
                    mx = jnp.maximum(mx, jnp.maximum(s[:, :LANES], s[:, LANES:]))
                return mx

            mx = lax.fori_loop(0, n_iter, max_body, jnp.full((tq, LANES), -jnp.inf, F32))
            shift_ref[...] = jnp.broadcast_to(jnp.max(mx, axis=-1, keepdims=True), (tq, LANES))

        m = shift_ref[...]
        m_hi = m.astype(BF16).astype(F32)
        m_lo = m - m_hi
        qa = jnp.where(lane == MAX_HI_COL, -m_hi, jnp.where(lane == MAX_LO_COL, -m_lo, q.astype(F32)))
        qa = qa.astype(BF16)

        def acc_body(c, carry, j=j, qa=qa):
            l, acc = carry
            for u in range(unroll):
                start = pl.multiple_of((c * unroll + u) * KEY_UNIT, KEY_UNIT)
                p = jnp.exp2(_qk(qa, k_ref, j, start))
                l = l + (p[:, :LANES] + p[:, LANES:])
                acc = acc + _dot(p.astype(BF16), v_ref[0, pl.ds(start, KEY_UNIT), :])
            return l, acc

        zeros = jnp.zeros((tq, LANES), F32)
        l, acc = lax.fori_loop(0, n_iter, acc_body, (zeros, zeros))
        outs.append(acc / _row_sums(l, 3))
    o_ref[...] = jnp.where(lane < V_HEAD, outs[0], outs[1]).astype(o_ref.dtype)


def _rope_tables(seq_len):
    pos = jnp.arange(seq_len, dtype=F32)
    inv_freq = 1.0 / (ROPE_THETA ** (jnp.arange(0, QK_ROPE, 2, dtype=F32) / QK_ROPE))
    ang = pos[:, None] * inv_freq[None, :]
    cos, sin = jnp.cos(ang), jnp.sin(ang)
    ones = jnp.ones((seq_len, QK_NOPE), F32)
    zeros_n = jnp.zeros((seq_len, QK_NOPE), F32)
    zeros_p = jnp.zeros((seq_len, HEAD_PAD - QK_NOPE - QK_ROPE), F32)
    c_tab = jnp.concatenate([ones, cos, cos, zeros_p], axis=1)
    s_tab = jnp.concatenate([zeros_n, -sin, sin, zeros_p], axis=1)
    return c_tab, s_tab


def _swap_halves(w):
    half = w.shape[-1] // 2
    return jnp.concatenate([w[..., half:], w[..., :half]], axis=-1)


def _mla(x, groups, g, wq_a, gq, wq_b, wkv_a, gkv, wkv_b, *, tm, tq, unroll):
    T, D = x.shape
    pad = HEAD_PAD - QK_NOPE - QK_ROPE
    wq3 = wq_b.reshape(Q_LORA, HEADS, QK_NOPE + QK_ROPE)
    qn, qr = wq3[..., :QK_NOPE], wq3[..., QK_NOPE:]
    zp = jnp.zeros((Q_LORA, HEADS, pad), F32)
    wq1 = jnp.concatenate([qn, qr, zp], -1).reshape(Q_LORA, HEADS * HEAD_PAD).astype(BF16)
    wq2 = jnp.concatenate([jnp.zeros_like(qn), _swap_halves(qr), zp], -1).reshape(Q_LORA, HEADS * HEAD_PAD).astype(BF16)
    wkva_c = wkv_a[:, :KV_LORA].astype(BF16)
    wr = wkv_a[:, KV_LORA:]
    zn = jnp.zeros((D, QK_NOPE), F32)
    zpd = jnp.zeros((D, pad), F32)
    wkr = jnp.concatenate([zn, wr, zpd, zn, _swap_halves(wr), zpd], axis=1).astype(BF16)
    wkv3 = wkv_b.reshape(KV_LORA, HEADS, QK_NOPE + V_HEAD)
    wk = jnp.concatenate([wkv3[..., :QK_NOPE], jnp.zeros((KV_LORA, HEADS, HEAD_PAD - QK_NOPE), F32)], -1)
    wk = wk.reshape(KV_LORA, HEADS * HEAD_PAD).astype(BF16)
    wv = wkv3[..., QK_NOPE:].reshape(KV_LORA, HEADS * V_HEAD).astype(BF16)

    s_max = max(s for _, _, s in groups)
    c_tab, s_tab = _rope_tables(s_max)
    cq_tab, sq_tab = c_tab * (ATTN_SCALE * LOG2E), s_tab * (ATTN_SCALE * LOG2E)
    kone = jnp.zeros((1, HEAD_PAD), F32).at[0, MAX_HI_COL:MAX_LO_COL + 1].set(1.0)

    bounds = [(off // tm, (off + b * s) // tm, s // tm) for off, b, s in groups]

    def pos_block(i):
        blk = i
        for lo, _, per in bounds:
            blk = jnp.where(i >= lo, (i - lo) % per, blk)
        return (blk, 0)

    for off, b, s in groups:
        assert s % tm == 0 and off % tm == 0
    tab_spec = pl.BlockSpec((tm, HEAD_PAD), pos_block)
    q, k, v = pl.pallas_call(
        _mla_proj_kernel,
        grid=(T // tm,),
        in_specs=[pl.BlockSpec((tm, D), lambda i: (i, 0)), _resident((1, D)),
                  _resident(wq_a.shape), _resident((1, Q_LORA)), _resident(wq1.shape), _resident(wq2.shape),
                  _resident(wkva_c.shape), _resident((1, KV_LORA)), _resident(wkr.shape),
                  _resident(wk.shape), _resident(wv.shape),
                  tab_spec, tab_spec, tab_spec, tab_spec, _resident((1, HEAD_PAD))],
        out_specs=[pl.BlockSpec((HEADS, tm, HEAD_PAD), lambda i: (0, i, 0)),
                   pl.BlockSpec((HEADS, tm, HEAD_PAD), lambda i: (0, i, 0)),
                   pl.BlockSpec((HEADS // 2, tm, LANES), lambda i: (0, i, 0))],
        out_shape=[jax.ShapeDtypeStruct((HEADS, T, HEAD_PAD), BF16),
                   jax.ShapeDtypeStruct((HEADS, T, HEAD_PAD), BF16),
                   jax.ShapeDtypeStruct((HEADS // 2, T, LANES), BF16)],
        compiler_params=_params(1, 48),
        name="mla_proj",
    )(x, g.reshape(1, D), wq_a.astype(BF16), gq.reshape(1, Q_LORA), wq1, wq2,
      wkva_c, gkv.reshape(1, KV_LORA), wkr, wk, wv, cq_tab, sq_tab, c_tab, s_tab, kone)

    outs = []
    for off, nb, s in groups:
        n_unroll = min(unroll, s // KEY_UNIT)
        tq_g = min(tq, s)
        assert off % s == 0 and s % tq_g == 0 and s % (KEY_UNIT * n_unroll) == 0
        seq0 = off // s
        nq = s // tq_g
        q0 = off // tq_g
        in_specs = [pl.BlockSpec((2, tq_g, HEAD_PAD), lambda b, hp, qi: (hp, q0 + b * nq + qi, 0)),
                    pl.BlockSpec((2, s, HEAD_PAD), lambda b, hp, qi: (hp, seq0 + b, 0)),
                    pl.BlockSpec((1, s, LANES), lambda b, hp, qi: (hp, seq0 + b, 0))]
        outs.append(pl.pallas_call(
            functools.partial(_attn_kernel, seq=s, unroll=n_unroll),
            grid=(nb, HEADS // 2, nq),
            in_specs=in_specs,
            out_specs=pl.BlockSpec((tq_g, LANES), lambda b, hp, qi: (b * nq + qi, hp)),
            out_shape=jax.ShapeDtypeStruct((nb * s, HEADS * V_HEAD), BF16),
            scratch_shapes=[pltpu.VMEM((2, SUBLANES, LANES), F32), pltpu.VMEM((tq_g, LANES), F32)],
            compiler_params=_params(3, 48),
            name="mla_attn",
        )(q, k, v))
    return outs


def _proj_res_kernel(x_ref, a_ref, w_ref, o_ref):
    o_ref[...] = x_ref[...] + _dot(a_ref[...], w_ref[...])


def _proj_res(x, a, w, *, tm):
    T, D = x.shape
    return pl.pallas_call(
        _proj_res_kernel,
        grid=(T // tm,),
        in_specs=[pl.BlockSpec((tm, D), lambda i: (i, 0)),
                  pl.BlockSpec((tm, a.shape[1]), lambda i: (i, 0)), _resident(w.shape)],
        out_specs=pl.BlockSpec((tm, D), lambda i: (i, 0)),
        out_shape=jax.ShapeDtypeStruct((T, D), F32),
        compiler_params=_params(1, 40),
        name="proj_res",
    )(x, a, w)


def _dft_cos_sin(n):
    idx = np.arange(n)
    ang = 2.0 * np.pi * ((idx[:, None] * idx[None, :]) % n) / n
    scale = 1.0 / math.sqrt(n)
    return np.cos(ang) * scale, np.sin(ang) * scale


def _fnet_factor(s):
    na = 1 << (int(math.log2(s)) // 2)
    nb = s // na
    assert na * nb == s and nb % (2 * SUBLANES) == 0 and na % SUBLANES == 0
    return na, nb


def _fnet_stage1_kernel(x_ref, g_ref, cc_ref, nsc_ref, k1c_ref, k1s_ref, twc_ref, tws_ref,
                        ar_ref, ai_ref):
    na = x_ref.shape[0]
    rows = na * SUBLANES
    halves_r, halves_i = [], []
    for half in range(2):
        sub = slice(half * SUBLANES, (half + 1) * SUBLANES)
        xb = x_ref[:, sub, :].reshape(rows, D_MODEL)
        h = _rms(xb, g_ref[...]).astype(BF16)
        twc = twc_ref[:, sub, :].reshape(rows, LANES)
        tws = tws_ref[:, sub, :].reshape(rows, LANES)
        twc = jnp.concatenate([twc] * (FNET_GC // LANES), axis=1)
        tws = jnp.concatenate([tws] * (FNET_GC // LANES), axis=1)
        cols_r, cols_i = [], []
        for grp in range(FNET_GROUPS):
            hg = h[:, grp * FNET_GC:(grp + 1) * FNET_GC]
            zr = _dot(hg, cc_ref[...]).astype(BF16)
            zi = _dot(hg, nsc_ref[...]).astype(BF16)
            k1c = k1c_ref[...]
            k1s = k1s_ref[...]
            a_r = _dot(k1c, zr) + _dot(k1s, zi)
            a_i = _dot(k1c, zi) - _dot(k1s, zr)
            cols_r.append(a_r * twc + a_i * tws)
            cols_i.append(a_i * twc - a_r * tws)
        halves_r.append(jnp.concatenate(cols_r, axis=1).reshape(na, SUBLANES, D_MODEL))
        halves_i.append(jnp.concatenate(cols_i, axis=1).reshape(na, SUBLANES, D_MODEL))
    ar_ref[...] = jnp.concatenate(halves_r, axis=1).astype(BF16)
    ai_ref[...] = jnp.concatenate(halves_i, axis=1).astype(BF16)


def _fnet_stage2_kernel(x_ref, ar_ref, ai_ref, gc_ref, gs_ref, wo_ref, bo_ref, o_ref):
    nb = x_ref.shape[0]
    rows = nb * SUBLANES
    y = _dot(gc_ref[...], ar_ref[...]) + _dot(gs_ref[...], ai_ref[...])
    m = _dot(y.astype(BF16), wo_ref[...]) + bo_ref[...]
    o_ref[...] = (x_ref[...].reshape(rows, D_MODEL) + m).reshape(nb, SUBLANES, D_MODEL)


def _fnet(x, groups, g, wo, bo):
    T, D = x.shape
    cc, sc = _dft_cos_sin(FNET_GC)
    cc_j = jnp.asarray(cc, F32).astype(BF16)
    nsc_j = jnp.asarray(-sc, F32).astype(BF16)
    wo_b = wo.astype(BF16)
    for off, nbatch, s in groups:
        assert off % s == 0
        na, nb = _fnet_factor(s)
        seq0 = off // s
        c1, s1 = _dft_cos_sin(na)
        eye = np.eye(SUBLANES)
        k1c = jnp.asarray(np.kron(c1, eye), F32).astype(BF16)
        k1s = jnp.asarray(np.kron(s1, eye), F32).astype(BF16)
        c2, s2 = _dft_cos_sin(nb)
        gc = np.zeros((nb, SUBLANES, SUBLANES, nb))
        gs = np.zeros((nb, SUBLANES, SUBLANES, nb))
        for j in range(SUBLANES):
            gc[:, j, j, :] = c2
            gs[:, j, j, :] = s2
        n2 = nb * SUBLANES
        gc_j = jnp.asarray(gc.reshape(n2, n2), F32).astype(BF16)
        gs_j = jnp.asarray(gs.reshape(n2, n2), F32).astype(BF16)
        k1i = jnp.arange(na, dtype=jnp.int32)[:, None]
        bi = jnp.arange(nb, dtype=jnp.int32)[None, :]
        ang = ((k1i * bi) % s).astype(F32) * (2.0 * math.pi / s)
        twc = jnp.broadcast_to(jnp.cos(ang)[:, :, None], (na, nb, LANES))
        tws = jnp.broadcast_to(jnp.sin(ang)[:, :, None], (na, nb, LANES))

        tb = 2 * SUBLANES
        n1 = na * SUBLANES
        x3 = x.reshape(T // nb, nb, D)
        blk3 = pl.BlockSpec((na, tb, D), lambda b, j: (seq0 + b, j, 0))
        tw_spec = pl.BlockSpec((na, tb, LANES), lambda b, j: (0, j, 0))
        a_r, a_i = pl.pallas_call(
            _fnet_stage1_kernel,
            grid=(nbatch, nb // tb),
            in_specs=[blk3, _resident((1, D)), _resident(cc_j.shape), _resident(nsc_j.shape),
                      _resident((n1, n1)), _resident((n1, n1)), tw_spec, tw_spec],
            out_specs=[pl.BlockSpec((na, tb, D), lambda b, j: (b, j, 0))] * 2,
            out_shape=[jax.ShapeDtypeStruct((nbatch * na, nb, D), BF16)] * 2,
            compiler_params=_params(2, 56),
            name="fnet_stage1",
        )(x3, g.reshape(1, D), cc_j, nsc_j, k1c, k1s, twc, tws)

        a_r = a_r.reshape(nbatch * s, D)
        a_i = a_i.reshape(nbatch * s, D)
        xo = x.reshape(T // na, na, D)
        xblk = pl.BlockSpec((nb, SUBLANES, D), lambda b, j: (seq0 + b, j, 0))
        ablk = pl.BlockSpec((n2, D), lambda b, j: (b * (na // SUBLANES) + j, 0))
        xo = pl.pallas_call(
            _fnet_stage2_kernel,
            grid=(nbatch, na // SUBLANES),
            in_specs=[xblk, ablk, ablk, _resident((n2, n2)), _resident((n2, n2)),
                      _resident(wo_b.shape), _resident((1, D))],
            out_specs=xblk,
            out_shape=jax.ShapeDtypeStruct(xo.shape, F32),
            input_output_aliases={0: 0},
            compiler_params=_params(2, 56),
            name="fnet_stage2",
        )(xo, a_r, a_i, gc_j, gs_j, wo_b, bo.reshape(1, D))
        x = xo.reshape(T, D)
    return x


def _gelu_tanh(x):
    return 0.5 * x * (1.0 + jnp.tanh(math.sqrt(2.0 / math.pi) * (x + 0.044715 * (x * x * x))))


def _sgu_kernel(x_ref, g_ref, wu_ref, wv_ref, gv_ref, ws_ref, bs_ref, wout_ref, o_ref):
    x = x_ref[...]
    tm = x.shape[0]
    h = _rms(x, g_ref[...]).astype(BF16)
    v = _gelu_tanh(_dot(h, wv_ref[...]))
    v = _rms(v, gv_ref[...]).astype(BF16)
    u = _gelu_tanh(_dot(h, wu_ref[...]))
    rows = []
    for c in range(tm // SGU_CHUNK):
        r = slice(c * SGU_CHUNK, (c + 1) * SGU_CHUNK)
        cols = []
        for grp in range(SGU_GROUPS):
            cs = slice(grp * SGU_GC, (grp + 1) * SGU_GC)
            sv = _dot(ws_ref[grp], v[r, cs]) + bs_ref[grp]
            cols.append((u[r, cs] * sv).astype(BF16))
        rows.append(jnp.concatenate(cols, axis=1))
    gated = jnp.concatenate(rows, axis=0)
    o_ref[...] = x + _dot(gated, wout_ref[...])


def _sgu(x, g, w_in, gv, ws, bs, w_out, *, tm):
    T, D = x.shape
    assert tm % SGU_CHUNK == 0
    wu = w_in[:, :SGU_HALF].astype(BF16)
    wv = w_in[:, SGU_HALF:].astype(BF16)
    bs_b = jnp.broadcast_to(bs[:, :, None], (SGU_GROUPS, SGU_CHUNK, SGU_GC)).astype(F32)
    return pl.pallas_call(
        _sgu_kernel,
        grid=(T // tm,),
        in_specs=[pl.BlockSpec((tm, D), lambda i: (i, 0)), _resident((1, D)),
                  _resident(wu.shape), _resident(wv.shape), _resident((1, SGU_HALF)),
                  _resident(ws.shape), _resident(bs_b.shape), _resident(w_out.shape)],
        out_specs=pl.BlockSpec((tm, D), lambda i: (i, 0)),
        out_shape=jax.ShapeDtypeStruct((T, D), F32),
        compiler_params=_params(1, 52),
        name="sgu",
    )(x, g.reshape(1, D), wu, wv, gv.reshape(1, SGU_HALF), ws.astype(BF16), bs_b, w_out.astype(BF16))


def _conv_kernel(x_ref, xp_ref, xn_ref, g_ref, w1_ref, b1_ref, wdw_ref, bdw_ref, gn_ref, w2_ref, b2_ref,
                 o_ref, glu_ref, sh_ref, dw_ref, *, seq_starts, seq_ends):
    tm = x_ref.shape[0]
    i = pl.program_id(0)
    start = i * tm
    is_first = functools.reduce(jnp.logical_or, [start == s for s in seq_starts])
    is_last = functools.reduce(jnp.logical_or, [start + tm == e for e in seq_ends])
    x = x_ref[...]
    xa = jnp.concatenate([xp_ref[...], x, xn_ref[...]], axis=0)
    h = _rms(xa, g_ref[...]).astype(BF16)
    row = lax.broadcasted_iota(jnp.int32, (tm + 2 * HALO, 1), 0)
    outside = jnp.logical_or(jnp.logical_and(is_first, row < HALO),
                             jnp.logical_and(is_last, row >= tm + HALO))
    n_sh = tm + 2 * HALO - SUBLANES
    for cb in range(D_MODEL // CONV_COLS):
        cs = slice(cb * CONV_COLS, (cb + 1) * CONV_COLS)
        gs = slice(D_MODEL + cb * CONV_COLS, D_MODEL + (cb + 1) * CONV_COLS)
        val = _dot(h, w1_ref[:, cs]) + b1_ref[:, cs]
        gate = _dot(h, w1_ref[:, gs]) + b1_ref[:, gs]
        glu_ref[cb] = jnp.where(outside, 0.0, val * jax.nn.sigmoid(gate))
        for b in range(SUBLANES):
            sh_ref[b] = glu_ref[cb, pl.ds(b, n_sh), :]
        acc = jnp.zeros((tm, CONV_COLS), F32) + bdw_ref[:, cs]
        for j in range(CONV_WIDTH):
            a8, b = divmod(HALO - CONV_PAD + j, SUBLANES)
            acc = acc + sh_ref[b, pl.ds(a8 * SUBLANES, tm), :] * wdw_ref[j:j + 1, cs]
        dw_ref[:, cs] = acc
    y = _rms(dw_ref[...], gn_ref[...])
    y = (y * jax.nn.sigmoid(y)).astype(BF16)
    o_ref[...] = x + _dot(y, w2_ref[...]) + b2_ref[...]


def _conv(x, groups, g, w_pw1, b_pw1, w_dw, b_dw, gn, w_pw2, b_pw2, *, tm):
    T, D = x.shape
    seq_starts = tuple(off + b * s for off, nb, s in groups for b in range(nb))
    seq_ends = tuple(off + (b + 1) * s for off, nb, s in groups for b in range(nb))
    for off, nb, s in groups:
        assert s % tm == 0 and off % tm == 0
    r = tm // HALO
    last_halo = T // HALO - 1
    kern = functools.partial(_conv_kernel, seq_starts=seq_starts, seq_ends=seq_ends)
    return pl.pallas_call(
        kern,
        grid=(T // tm,),
        in_specs=[pl.BlockSpec((tm, D), lambda i: (i, 0)),
                  pl.BlockSpec((HALO, D), lambda i: (jnp.maximum(i * r - 1, 0), 0)),
                  pl.BlockSpec((HALO, D), lambda i: (jnp.minimum((i + 1) * r, last_halo), 0)),
                  _resident((1, D)), _resident(w_pw1.shape), _resident((1, 2 * D)),
                  _resident((CONV_WIDTH, D)), _resident((1, D)), _resident((1, D)),
                  _resident(w_pw2.shape), _resident((1, D))],
        out_specs=pl.BlockSpec((tm, D), lambda i: (i, 0)),
        out_shape=jax.ShapeDtypeStruct((T, D), F32),
        scratch_shapes=[pltpu.VMEM((D // CONV_COLS, tm + 2 * HALO, CONV_COLS), F32),
                        pltpu.VMEM((SUBLANES, tm + 2 * HALO - SUBLANES, CONV_COLS), F32),
                        pltpu.VMEM((tm, D), F32)],
        compiler_params=_params(1, 48),
        name="conv",
    )(x, x, x, g.reshape(1, D), w_pw1.astype(BF16), b_pw1.reshape(1, 2 * D),
      w_dw.reshape(CONV_WIDTH, D), b_dw.reshape(1, D), gn.reshape(1, D),
      w_pw2.astype(BF16), b_pw2.reshape(1, D))


def kernel(x_prompt, x_sample, norm_g, final_g, ffn1_wg, ffn1_wu, ffn1_wd, ffn2_wg, ffn2_wu, ffn2_wd,
           mla_wq_a, mla_gq, mla_wq_b, mla_wkv_a, mla_gkv, mla_wkv_b, mla_wo,
           fnet_wo, fnet_bo, sgu_w_in, sgu_gv, sgu_ws, sgu_bs, sgu_w_out,
           conv_w_pw1, conv_b_pw1, conv_w_dw, conv_b_dw, conv_gn, conv_w_pw2, conv_b_pw2):
    bp, sp, D = x_prompt.shape
    bs_, ss, _ = x_sample.shape
    assert D == D_MODEL
    n_p, n_s = bp * sp, bs_ * ss
    groups = ((0, bp, sp), (n_p, bs_, ss))
    n_tok = n_p + n_s
    tm = 512
    depth = norm_g.shape[0]
    assert depth == 4

    def ffn_w(wg, wu, wd, i):
        return wg[i].astype(BF16), wu[i].astype(BF16), wd[i].astype(BF16)

    def ffn(x, *a, **kw):
        return _ffn(x, *a, tm=FFN_TM, **kw)

    x = _ffn(x_prompt.reshape(n_p, D), norm_g[0, 0], *ffn_w(ffn1_wg, ffn1_wu, ffn1_wd, 0),
             tm=tm, second=x_sample.reshape(n_s, D))
    attn = _mla(x, groups, norm_g[0, 1], mla_wq_a[0], mla_gq[0], mla_wq_b[0], mla_wkv_a[0], mla_gkv[0],
                mla_wkv_b[0], **MLA_TILES)
    w2, wo = ffn_w(ffn2_wg, ffn2_wu, ffn2_wd, 0), mla_wo[0].astype(BF16)
    for (off, nb, s), o in zip(groups, attn):
        x = ffn(x, norm_g[0, 2], *w2, row_start=off, n_rows=nb * s, pre=(o, wo), in_place=True)
    x = ffn(x, norm_g[1, 0], *ffn_w(ffn1_wg, ffn1_wu, ffn1_wd, 1))
    x = _fnet(x, groups, norm_g[1, 1], fnet_wo[0], fnet_bo[0])
    x = ffn(x, norm_g[1, 2], *ffn_w(ffn2_wg, ffn2_wu, ffn2_wd, 1))
    x = ffn(x, norm_g[2, 0], *ffn_w(ffn1_wg, ffn1_wu, ffn1_wd, 2))
    x = _sgu(x, norm_g[2, 1], sgu_w_in[0], sgu_gv[0], sgu_ws[0], sgu_bs[0], sgu_w_out[0], tm=tm)
    x = ffn(x, norm_g[2, 2], *ffn_w(ffn2_wg, ffn2_wu, ffn2_wd, 2))
    x = ffn(x, norm_g[3, 0], *ffn_w(ffn1_wg, ffn1_wu, ffn1_wd, 3))
    x = _conv(x, groups, norm_g[3, 1], conv_w_pw1[0], conv_b_pw1[0], conv_w_dw[0], conv_b_dw[0],
              conv_gn[0], conv_w_pw2[0], conv_b_pw2[0], tm=tm)
    w3 = ffn_w(ffn2_wg, ffn2_wu, ffn2_wd, 3)
    y_p = ffn(x, norm_g[3, 2], *w3, row_start=0, n_rows=n_p, final_g=final_g)
    y_s = ffn(x, norm_g[3, 2], *w3, row_start=n_p, n_rows=n_s, final_g=final_g)
    return y_p.reshape(bp, sp, D), y_s.reshape(bs_, ss, D)
```
